```python
import math
import jax, jax.numpy as jnp
from jax import lax
import numpy as np

D_MODEL = 2048
BATCH = 2
SEQ = 16384
DEPTH = 1

D_MIX = D_MODEL
C_CONV = D_MIX // 2
CONV_GROUP_DIM = 128
N_CONV_GROUPS = C_CONV // CONV_GROUP_DIM
CONV_WIDTH = 31
D_DN = D_MIX - C_CONV
DN_HEAD_DIM = 128
N_DN_HEADS = D_DN // DN_HEAD_DIM
SHORT_CONV_WIDTH = 4
CHUNK = 64
D_FF = 4 * D_MODEL
EPS = 1e-6

OFF_CONF = 0
OFF_QKV = OFF_CONF + 2 * C_CONV
OFF_Z = OFF_QKV + 3 * D_DN
OFF_B = OFF_Z + D_DN
OFF_A = OFF_B + N_DN_HEADS
D_IN = OFF_A + N_DN_HEADS

kernel_name = "hybrid_conformer_gated_deltanet_block"


def rms_norm(x, w):
    xf = x.astype(jnp.float32)
    y = xf * lax.rsqrt(jnp.mean(xf * xf, axis=-1, keepdims=True) + EPS)
    return (y * w.astype(jnp.float32)).astype(x.dtype)


def causal_depthwise_conv(x, w):
    k_width, ch = w.shape
    return lax.conv_general_dilated(
        x, w[:, None, :].astype(x.dtype), window_strides=(1,),
        padding=[(k_width - 1, 0)], dimension_numbers=("NWC", "WIO", "NWC"),
        feature_group_count=ch)


def l2_normalize(x):
    return x * lax.rsqrt(jnp.sum(x * x, axis=-1, keepdims=True) + EPS)


def conformer_conv_group(p, b_glu, dw_w, dw_b, ln_g, ln_b):
    bsz, seq, _ = p.shape
    p = p + b_glu.astype(p.dtype)
    h = p[..., :C_CONV] * jax.nn.sigmoid(p[..., C_CONV:])
    h = causal_depthwise_conv(h, dw_w) + dw_b.astype(h.dtype)
    hg = h.astype(jnp.float32).reshape(bsz, seq, N_CONV_GROUPS, CONV_GROUP_DIM)
    mu = jnp.mean(hg, axis=-1, keepdims=True)
    var = jnp.mean(jnp.square(hg - mu), axis=-1, keepdims=True)
    hn = ((hg - mu) * lax.rsqrt(var + EPS)).reshape(bsz, seq, C_CONV)
    hn = hn * ln_g.astype(jnp.float32) + ln_b.astype(jnp.float32)
    return jax.nn.silu(hn).astype(p.dtype)


def chunked_gated_delta_rule(q, k, v, g, beta):
    bsz, seq, nh, dk = q.shape
    dv = v.shape[-1]
    nc = seq // CHUNK
    q = q * (dk ** -0.5)

    def to_chunks(t):
        return t.reshape(bsz, nc, CHUNK, nh, -1).transpose(0, 3, 1, 2, 4)

    q, k, v = to_chunks(q), to_chunks(k), to_chunks(v)
    g = g.reshape(bsz, nc, CHUNK, nh).transpose(0, 3, 1, 2)
    beta = beta.reshape(bsz, nc, CHUNK, nh).transpose(0, 3, 1, 2)
    g = jnp.cumsum(g, axis=-1)

    causal = jnp.tril(jnp.ones((CHUNK, CHUNK), dtype=bool))
    strict = jnp.tril(jnp.ones((CHUNK, CHUNK), dtype=bool), -1)
    decay = jnp.exp(jnp.where(causal, g[..., :, None] - g[..., None, :], -jnp.inf))

    k_beta = k * beta[..., None]
    v_beta = v * beta[..., None]
    a = jnp.where(strict, jnp.einsum("bhncd,bhnsd->bhncs", k_beta, k) * decay, 0.0)
    eye = jnp.eye(CHUNK, dtype=jnp.float32)
    t_mat = lax.linalg.triangular_solve(eye + a, jnp.broadcast_to(eye, a.shape),
                                        left_side=True, lower=True, unit_diagonal=True)
    u = jnp.einsum("bhncs,bhnsd->bhncd", t_mat, v_beta)
    w = jnp.einsum("bhncs,bhnsd->bhncd", t_mat, k_beta * jnp.exp(g)[..., None])
    q_g = q * jnp.exp(g)[..., None]
    k_g = k * jnp.exp(g[..., -1:] - g)[..., None]
    last_decay = jnp.exp(g[..., -1])
    intra = jnp.einsum("bhncd,bhnsd->bhncs", q, k) * decay

    def step(state, inp):
        w_i, u_i, qg_i, kg_i, ld_i = inp
        v_new = u_i - jnp.einsum("bhcd,bhde->bhce", w_i, state)
        o_inter = jnp.einsum("bhcd,bhde->bhce", qg_i, state)
        state = state * ld_i[..., None, None] + jnp.einsum("bhcd,bhce->bhde", kg_i, v_new)
        return state, (v_new, o_inter)

    xs = tuple(jnp.moveaxis(t, 2, 0) for t in (w, u, q_g, k_g, last_decay))
    s0 = jnp.zeros((bsz, nh, dk, dv), jnp.float32)
    _, (v_new, o_inter) = lax.scan(step, s0, xs)
    v_new = jnp.moveaxis(v_new, 0, 2)
    o_inter = jnp.moveaxis(o_inter, 0, 2)
    o = o_inter + jnp.einsum("bhncs,bhnse->bhnce", intra, v_new)
    return o.transpose(0, 2, 3, 1, 4).reshape(bsz, seq, nh, dv)


def gated_deltanet_group(p, conv_w, a_log, dt_bias, norm_w):
    bsz, seq, _ = p.shape
    dtype = p.dtype
    qkv = jax.nn.silu(causal_depthwise_conv(p[..., :3 * D_DN], conv_w))
    qkv = qkv.astype(jnp.float32).reshape(bsz, seq, 3, N_DN_HEADS, DN_HEAD_DIM)
    q = l2_normalize(qkv[:, :, 0])
    k = l2_normalize(qkv[:, :, 1])
    v = qkv[:, :, 2]
    z = p[..., 3 * D_DN:4 * D_DN].astype(jnp.float32).reshape(bsz, seq, N_DN_HEADS, DN_HEAD_DIM)
    b_raw = p[..., 4 * D_DN:4 * D_DN + N_DN_HEADS].astype(jnp.float32)
    a_raw = p[..., 4 * D_DN + N_DN_HEADS:4 * D_DN + 2 * N_DN_HEADS].astype(jnp.float32)
    beta = jax.nn.sigmoid(b_raw)
    g = -jnp.exp(a_log.astype(jnp.float32)) * jax.nn.softplus(a_raw + dt_bias.astype(jnp.float32))
    o = chunked_gated_delta_rule(q, k, v, g, beta)
    o = o * lax.rsqrt(jnp.mean(o * o, axis=-1, keepdims=True) + EPS) * norm_w.astype(jnp.float32)
    o = o * jax.nn.silu(z)
    return o.reshape(bsz, seq, D_DN).astype(dtype)


def setup_inputs(seed: int = 0) -> dict:
    key = jax.random.key(seed)
    ks = jax.random.split(key, 20)
    f32 = jnp.float32
    x = jax.random.normal(ks[0], (BATCH, SEQ, D_MODEL), f32)
    norm1_w = 1.0 + 0.02 * jax.random.normal(ks[1], (DEPTH, D_MODEL), f32)
    w_in = jax.random.normal(ks[2], (DEPTH, D_MODEL, D_IN), f32) * D_MODEL ** -0.5
    b_glu = 0.02 * jax.random.normal(ks[3], (DEPTH, 2 * C_CONV), f32)
    conf_dw_w = jax.random.normal(ks[4], (DEPTH, CONV_WIDTH, C_CONV), f32) * CONV_WIDTH ** -0.5
    conf_dw_b = 0.02 * jax.random.normal(ks[5], (DEPTH, C_CONV), f32)
    conf_ln_g = 1.0 + 0.02 * jax.random.normal(ks[6], (DEPTH, C_CONV), f32)
    conf_ln_b = 0.02 * jax.random.normal(ks[7], (DEPTH, C_CONV), f32)
    dn_conv_w = jax.random.normal(ks[8], (DEPTH, SHORT_CONV_WIDTH, 3 * D_DN), f32) * SHORT_CONV_WIDTH ** -0.5
    dn_a_log = jnp.log(jax.random.uniform(ks[9], (DEPTH, N_DN_HEADS), f32, 1.0, 16.0))
    dt = jnp.exp(jax.random.uniform(ks[10], (DEPTH, N_DN_HEADS), f32, math.log(1e-3), math.log(1e-1)))
    dn_dt_bias = dt + jnp.log(-jnp.expm1(-dt))
    dn_norm_w = 1.0 + 0.02 * jax.random.normal(ks[11], (DEPTH, DN_HEAD_DIM), f32)
    w_out = jax.random.normal(ks[12], (DEPTH, D_MIX, D_MODEL), f32) * D_MIX ** -0.5
    norm2_w = 1.0 + 0.02 * jax.random.normal(ks[13], (DEPTH, D_MODEL), f32)
    w_mlp_up = jax.random.normal(ks[14], (DEPTH, D_MODEL, D_FF), f32) * D_MODEL ** -0.5
    w_mlp_down = jax.random.normal(ks[15], (DEPTH, D_FF, D_MODEL), f32) * D_FF ** -0.5
    final_norm_w = 1.0 + 0.02 * jax.random.normal(ks[16], (D_MODEL,), f32)
    return {"x": x, "norm1_w": norm1_w, "w_in": w_in, "b_glu": b_glu,
            "conf_dw_w": conf_dw_w, "conf_dw_b": conf_dw_b, "conf_ln_g": conf_ln_g,
            "conf_ln_b": conf_ln_b, "dn_conv_w": dn_conv_w, "dn_a_log": dn_a_log,
            "dn_dt_bias": dn_dt_bias, "dn_norm_w": dn_norm_w, "w_out": w_out,
            "norm2_w": norm2_w, "w_mlp_up": w_mlp_up, "w_mlp_down": w_mlp_down,
            "final_norm_w": final_norm_w}


def reference(x, norm1_w, w_in, b_glu, conf_dw_w, conf_dw_b, conf_ln_g, conf_ln_b,
              dn_conv_w, dn_a_log, dn_dt_bias, dn_norm_w, w_out, norm2_w,
              w_mlp_up, w_mlp_down, final_norm_w):
    h = x
    for l in range(DEPTH):
        u = rms_norm(h, norm1_w[l])
        proj = u @ w_in[l]
        conv_out = conformer_conv_group(proj[..., OFF_CONF:OFF_QKV], b_glu[l], conf_dw_w[l],
                                        conf_dw_b[l], conf_ln_g[l], conf_ln_b[l])
        dn_out = gated_deltanet_group(proj[..., OFF_QKV:], dn_conv_w[l], dn_a_log[l],
                                      dn_dt_bias[l], dn_norm_w[l])
        mix = jnp.concatenate([conv_out, dn_out], axis=-1)
        h = h + mix @ w_out[l]
        m = rms_norm(h, norm2_w[l]) @ w_mlp_up[l]
        h = h + jnp.square(jax.nn.relu(m)) @ w_mlp_down[l]
    return rms_norm(h, final_norm_w)
```

```python
import functools

import jax
import jax.numpy as jnp
from jax import lax
from jax.experimental import pallas as pl
from jax.experimental.pallas import tpu as pltpu

F32 = jnp.float32
BF16 = jnp.bfloat16

EPS = 1e-6
LANES = 128
CONV_WIDTH = 31
SHORT_CONV_WIDTH = 4
N_GROUPS = 8
CONF_HALO = 32
DN_HALO = 8
CHUNK = 128

CB_VAL, CB_GATE, CB_Q, CB_K, CB_V, CB_Z, CB_BA = 0, 8, 16, 24, 32, 40, 48
N_COLBLOCKS = 50

VMEM_LIMIT = 56 * 1024 * 1024


def _sigmoid(x):
    return 1.0 / (1.0 + jnp.exp(-x))


def _silu(x):
    return x * _sigmoid(x)


def _inproj_kernel(x_ref, nw_ref, w_ref, o_ref, u_scr):
    @pl.when(pl.program_id(1) == 0)
    def _():
        x = x_ref[...]
        ms = jnp.mean(x * x, axis=-1, keepdims=True)
        u_scr[...] = (x * lax.rsqrt(ms + EPS) * nw_ref[...]).astype(BF16)

    res = jnp.dot(u_scr[...], w_ref[...], preferred_element_type=F32)
    for c in range(o_ref.shape[0]):
        o_ref[c] = res[:, c * LANES:(c + 1) * LANES]


def _inproj(x2, norm_w, w_in_p, *, tm=1024, tn=1280):
    t, d = x2.shape
    n = w_in_p.shape[1]
    return pl.pallas_call(
        _inproj_kernel,
        out_shape=jax.ShapeDtypeStruct((n // LANES, t, LANES), F32),
        grid=(t // tm, n // tn),
        in_specs=[
            pl.BlockSpec((tm, d), lambda i, j: (i, 0)),
            pl.BlockSpec((1, d), lambda i, j: (0, 0)),
            pl.BlockSpec((d, tn), lambda i, j: (0, j)),
        ],
        out_specs=pl.BlockSpec((tn // LANES, tm, LANES), lambda i, j: (j, i, 0)),
        scratch_shapes=[pltpu.VMEM((tm, d), BF16)],
        compiler_params=pltpu.CompilerParams(
            dimension_semantics=("arbitrary", "arbitrary"),
            vmem_limit_bytes=VMEM_LIMIT),
        name="inproj",
    )(x2, norm_w, w_in_p)


def _conf_kernel(v_ref, g_ref, bv_ref, bg_ref, w_ref, cb_ref, lg_ref, lb_ref,
                 o_ref, h_scr, *, ts, rt):
    i = pl.program_id(2)

    @pl.when(i == 0)
    def _():
        h_scr[0:CONF_HALO, :] = jnp.zeros((CONF_HALO, LANES), F32)

    @pl.when(i > 0)
    def _():
        h_scr[0:CONF_HALO, :] = h_scr[ts:ts + CONF_HALO, :]

    a = v_ref[0] + bv_ref[0]
    g = g_ref[0] + bg_ref[0]
    h_scr[CONF_HALO:CONF_HALO + ts, :] = a * _sigmoid(g)

    first_tap = CONF_HALO - (CONV_WIDTH - 1)

    def body(r, carry):
        base = pl.multiple_of(r * rt, rt)
        acc = jnp.zeros((rt, LANES), F32)
        for k in range(CONV_WIDTH):
            acc = acc + w_ref[0, k:k + 1, :] * h_scr[pl.ds(base + first_tap + k, rt), :]
        acc = acc + cb_ref[0]
        mu = jnp.mean(acc, axis=-1, keepdims=True)
        cen = acc - mu
        var = jnp.mean(cen * cen, axis=-1, keepdims=True)
        hn = cen * lax.rsqrt(var + EPS) * lg_ref[0] + lb_ref[0]
        o_ref[0, pl.ds(base, rt), :] = _silu(hn).astype(o_ref.dtype)
        return carry

    lax.fori_loop(0, ts // rt, body, 0)


def _conf(proj3, b_glu, dw_w, dw_b, ln_g, ln_b, *, batch, seq, ts=2048, rt=64):
    t = proj3.shape[1]
    ns = seq // ts
    vec = lambda off: pl.BlockSpec((1, 1, LANES), lambda b, c, i: (off + c, 0, 0))
    return pl.pallas_call(
        functools.partial(_conf_kernel, ts=ts, rt=rt),
        out_shape=jax.ShapeDtypeStruct((N_GROUPS, t, LANES), BF16),
        grid=(batch, N_GROUPS, ns),
        in_specs=[
            pl.BlockSpec((1, ts, LANES), lambda b, c, i: (CB_VAL + c, b * ns + i, 0)),
            pl.BlockSpec((1, ts, LANES), lambda b, c, i: (CB_GATE + c, b * ns + i, 0)),
            vec(0), vec(N_GROUPS),
            pl.BlockSpec((1, CONV_WIDTH, LANES), lambda b, c, i: (c, 0, 0)),
            vec(0), vec(0), vec(0),
        ],
        out_specs=pl.BlockSpec((1, ts, LANES), lambda b, c, i: (c, b * ns + i, 0)),
        scratch_shapes=[pltpu.VMEM((CONF_HALO + ts, LANES), F32)],
        compiler_params=pltpu.CompilerParams(
            dimension_semantics=("arbitrary", "arbitrary", "arbitrary"),
            vmem_limit_bytes=VMEM_LIMIT),
        name="conf",
    )(proj3, proj3, b_glu, b_glu, dw_w, dw_b, ln_g, ln_b)


def _delta_kernel(q_ref, k_ref, v_ref, z_ref, ba_ref, cw_ref, alog_ref, dtb_ref, nw_ref,
                  o_ref,
                  x_scr, q_scr, k_scr, vb_scr, beta_scr, g_scr, mask_scr,
                  u_scr, wq_scr, kgt_scr, intra_scr, ld_scr, o_scr, s_scr, *, ts):
    h = pl.program_id(1)
    i = pl.program_id(2)
    nc = ts // CHUNK
    c2 = 2 * CHUNK

    r_io = lax.broadcasted_iota(jnp.int32, (CHUNK, CHUNK), 0)
    c_io = lax.broadcasted_iota(jnp.int32, (CHUNK, CHUNK), 1)
    causal = r_io >= c_io
    tri = jnp.where(causal, 1.0, 0.0).astype(F32)
    eye = jnp.where(r_io == c_io, 1.0, 0.0).astype(F32)
    n_levels = CHUNK.bit_length() - 1
    for lvl in range(n_levels):
        same = (r_io >> (lvl + 1)) == (c_io >> (lvl + 1))
        lower_left = jnp.where(((r_io >> lvl) & 1) == 1, 1.0, 0.0) * \
            jnp.where(((c_io >> lvl) & 1) == 0, 1.0, 0.0)
        mask_scr[lvl] = jnp.where(same, lower_left, 0.0).astype(F32)

    @pl.when(i == 0)
    def _():
        x_scr[:, 0:DN_HALO, :] = jnp.zeros((3, DN_HALO, LANES), F32)
        s_scr[...] = jnp.zeros_like(s_scr)

    @pl.when(i > 0)
    def _():
        x_scr[:, 0:DN_HALO, :] = x_scr[:, ts:ts + DN_HALO, :]

    first_tap = DN_HALO - (SHORT_CONV_WIDTH - 1)
    ys = []
    for j, ref in enumerate((q_ref, k_ref, v_ref)):
        x_scr[j, DN_HALO:DN_HALO + ts, :] = ref[0]
        y = jnp.zeros((ts, LANES), F32)
        for m in range(SHORT_CONV_WIDTH):
            y = y + cw_ref[m, j, pl.ds(h, 1), :] * x_scr[j, pl.ds(first_tap + m, ts), :]
        ys.append(_silu(y))
    qn = ys[0] * lax.rsqrt(jnp.sum(ys[0] * ys[0], axis=-1, keepdims=True) + EPS)
    kn = ys[1] * lax.rsqrt(jnp.sum(ys[1] * ys[1], axis=-1, keepdims=True) + EPS)

    ba = ba_ref[0]
    lane = lax.broadcasted_iota(jnp.int32, (ts, LANES), 1)
    beta_all = _sigmoid(ba)
    sp_in = ba + dtb_ref[...]
    softplus = jnp.maximum(sp_in, 0.0) + jnp.log1p(jnp.exp(-jnp.abs(sp_in)))
    g_all = -jnp.exp(alog_ref[...]) * softplus
    beta = jnp.sum(jnp.where(lane == h, beta_all, 0.0), axis=-1, keepdims=True)
    g = jnp.sum(jnp.where(lane == h + N_GROUPS, g_all, 0.0), axis=-1, keepdims=True)

    q_scr[...] = qn * (LANES ** -0.5)
    k_scr[...] = kn
    vb_scr[...] = ys[2] * beta
    beta_scr[...] = jnp.broadcast_to(beta, (ts, LANES))
    g_scr[...] = jnp.broadcast_to(g, (ts, LANES))

    def prep(c, carry):
        rows = pl.ds(pl.multiple_of(c * CHUNK, CHUNK), CHUNK)
        k_c = k_scr[rows, :]
        kb_c = k_c * beta_scr[rows, :]
        qs_c = q_scr[rows, :]
        cs = jnp.dot(tri, g_scr[rows, :], precision=lax.Precision.HIGHEST,
                     preferred_element_type=F32)
        decay = jnp.exp(jnp.where(causal, cs - cs.T, -jnp.inf))
        kq = lax.dot_general(
            jnp.concatenate([kb_c, qs_c], axis=0).astype(BF16), k_c.astype(BF16),
            (((1,), (1,)), ((), ())), preferred_element_type=F32)
        a_mat = kq[:CHUNK] * decay * (tri - eye)
        intra_scr[c] = (kq[CHUNK:] * decay).astype(BF16)
        x_inv = eye - a_mat * mask_scr[0]
        for lvl in range(1, n_levels):
            m_l = (a_mat * mask_scr[lvl]).astype(BF16)
            x_b = x_inv.astype(BF16)
            mx = jnp.dot(m_l, x_b, preferred_element_type=F32)
            x_inv = x_inv - jnp.dot(x_b, mx.astype(BF16), preferred_element_type=F32)
        eg = jnp.exp(cs)
        rhs = jnp.concatenate([vb_scr[rows, :], kb_c * eg], axis=1).astype(BF16)
        uw = jnp.dot(x_inv.astype(BF16), rhs, preferred_element_type=F32)
        u_scr[c] = uw[:, :LANES]
        wq_scr[c, 0:CHUNK, :] = uw[:, LANES:].astype(BF16)
        wq_scr[c, CHUNK:c2, :] = (qs_c * eg).astype(BF16)
        g_last = cs[CHUNK - 1:CHUNK, :]
        kgt_scr[c] = (k_c * jnp.exp(g_last - cs)).T.astype(BF16)
        ld_scr[c] = jnp.exp(g_last)
        return carry

    lax.fori_loop(0, nc, prep, 0)

    def step(c, state):
        ws = jnp.dot(wq_scr[c], state.astype(BF16), preferred_element_type=F32)
        v_new = (u_scr[c] - ws[:CHUNK]).astype(BF16)
        o_scr[c] = ws[CHUNK:] + jnp.dot(intra_scr[c], v_new, preferred_element_type=F32)
        return state * ld_scr[c] + jnp.dot(kgt_scr[c], v_new, preferred_element_type=F32)

    s_scr[...] = lax.fori_loop(0, nc, step, s_scr[...])

    o = o_scr[...].reshape(ts, LANES)
    o = o * lax.rsqrt(jnp.mean(o * o, axis=-1, keepdims=True) + EPS) * nw_ref[...]
    o_ref[0] = (o * _silu(z_ref[0])).astype(o_ref.dtype)


def _delta(proj3, conv_w4, alog_row, dtb_row, norm_row, *, batch, seq, ts=2048):
    t = proj3.shape[1]
    ns = seq // ts
    nc = ts // CHUNK
    col = lambda off: pl.BlockSpec((1, ts, LANES), lambda b, h, i: (off + h, b * ns + i, 0))
    row = pl.BlockSpec((1, LANES), lambda b, h, i: (0, 0))
    n_levels = CHUNK.bit_length() - 1
    return pl.pallas_call(
        functools.partial(_delta_kernel, ts=ts),
        out_shape=jax.ShapeDtypeStruct((N_GROUPS, t, LANES), BF16),
        grid=(batch, N_GROUPS, ns),
        in_specs=[
            col(CB_Q), col(CB_K), col(CB_V), col(CB_Z),
            pl.BlockSpec((1, ts, LANES), lambda b, h, i: (CB_BA, b * ns + i, 0)),
            pl.BlockSpec((SHORT_CONV_WIDTH, 3, N_GROUPS, LANES), lambda b, h, i: (0, 0, 0, 0)),
            row, row, row,
        ],
        out_specs=pl.BlockSpec((1, ts, LANES), lambda b, h, i: (h, b * ns + i, 0)),
        scratch_shapes=[
            pltpu.VMEM((3, DN_HALO + ts, LANES), F32),
            pltpu.VMEM((ts, LANES), F32),
            pltpu.VMEM((ts, LANES), F32),
            pltpu.VMEM((ts, LANES), F32),
            pltpu.VMEM((ts, LANES), F32),
            pltpu.VMEM((ts, LANES), F32),
            pltpu.VMEM((n_levels, CHUNK, CHUNK), F32),
            pltpu.VMEM((nc, CHUNK, LANES), F32),
            pltpu.VMEM((nc, 2 * CHUNK, LANES), BF16),
            pltpu.VMEM((nc, LANES, CHUNK), BF16),
            pltpu.VMEM((nc, CHUNK, CHUNK), BF16),
            pltpu.VMEM((nc, 1, LANES), F32),
            pltpu.VMEM((nc, CHUNK, LANES), F32),
            pltpu.VMEM((LANES, LANES), F32),
        ],
        compiler_params=pltpu.CompilerParams(
            dimension_semantics=("arbitrary", "arbitrary", "arbitrary"),
            vmem_limit_bytes=VMEM_LIMIT),
        name="delta",
    )(proj3, proj3, proj3, proj3, proj3, conv_w4, alog_row, dtb_row, norm_row)


def _mlp_kernel(x_ref, conv_ref, dn_ref, wout_ref, n2_ref, wup_ref, wdn_ref, fn_ref,
                o_ref, r_scr):
    f = pl.program_id(1)

    @pl.when(f == 0)
    def _():
        mix = jnp.concatenate([conv_ref[c] for c in range(N_GROUPS)]
                              + [dn_ref[c] for c in range(N_GROUPS)], axis=-1)
        h1 = x_ref[...] + jnp.dot(mix, wout_ref[...], preferred_element_type=F32)
        o_ref[...] = h1
        ms = jnp.mean(h1 * h1, axis=-1, keepdims=True)
        r_scr[...] = (h1 * lax.rsqrt(ms + EPS) * n2_ref[...]).astype(BF16)

    m = jnp.dot(r_scr[...], wup_ref[...], preferred_element_type=F32)
    act = jnp.square(jnp.maximum(m, 0.0)).astype(BF16)
    o_ref[...] += jnp.dot(act, wdn_ref[...], preferred_element_type=F32)

    @pl.when(f == pl.num_programs(1) - 1)
    def _():
        h2 = o_ref[...]
        ms = jnp.mean(h2 * h2, axis=-1, keepdims=True)
        o_ref[...] = h2 * lax.rsqrt(ms + EPS) * fn_ref[...]


def _mlp(x2, conv_o, dn_o, w_out_b, n2, w_up_b, w_dn_b, fn, *, tm=512, tf=1024):
    t, d = x2.shape
    dff = w_up_b.shape[1]
    return pl.pallas_call(
        _mlp_kernel,
        out_shape=jax.ShapeDtypeStruct((t, d), F32),
        grid=(t // tm, dff // tf),
        in_specs=[
            pl.BlockSpec((tm, d), lambda i, f: (i, 0)),
            pl.BlockSpec((N_GROUPS, tm, LANES), lambda i, f: (0, i, 0)),
            pl.BlockSpec((N_GROUPS, tm, LANES), lambda i, f: (0, i, 0)),
            pl.BlockSpec((d, d), lambda i, f: (0, 0), pipeline_mode=pl.Buffered(1)),
            pl.BlockSpec((1, d), lambda i, f: (0, 0)),
            pl.BlockSpec((d, tf), lambda i, f: (0, f)),
            pl.BlockSpec((tf, d), lambda i, f: (f, 0)),
            pl.BlockSpec((1, d), lambda i, f: (0, 0)),
        ],
        out_specs=pl.BlockSpec((tm, d), lambda i, f: (i, 0)),
        scratch_shapes=[pltpu.VMEM((tm, d), BF16)],
        compiler_params=pltpu.CompilerParams(
            dimension_semantics=("arbitrary", "arbitrary"),
            vmem_limit_bytes=VMEM_LIMIT),
        name="mlp",
    )(x2, conv_o, dn_o, w_out_b, n2, w_up_b, w_dn_b, fn)


def _lane_row(vals, offset):
    return jnp.zeros((1, LANES), F32).at[0, offset:offset + vals.shape[0]].set(vals.astype(F32))


def kernel(x, norm1_w, w_in, b_glu, conf_dw_w, conf_dw_b, conf_ln_g, conf_ln_b, dn_conv_w,
           dn_a_log, dn_dt_bias, dn_norm_w, w_out, norm2_w, w_mlp_up, w_mlp_down, final_norm_w):
    batch, seq, d = x.shape
    assert norm1_w.shape[0] == 1, "single-layer block"
    h = x.reshape(batch * seq, d)

    d_in = w_in.shape[2]
    w_in_p = jnp.pad(w_in[0], ((0, 0), (0, N_COLBLOCKS * LANES - d_in))).astype(BF16)
    proj3 = _inproj(h, norm1_w[0].reshape(1, d), w_in_p)

    grp = lambda v: v.reshape(-1, 1, LANES)
    dw_w = conf_dw_w[0].reshape(CONV_WIDTH, N_GROUPS, LANES).transpose(1, 0, 2)
    conv_o = _conf(proj3, grp(b_glu[0]), dw_w, grp(conf_dw_b[0]), grp(conf_ln_g[0]),
                   grp(conf_ln_b[0]), batch=batch, seq=seq)

    conv_w4 = dn_conv_w[0].reshape(SHORT_CONV_WIDTH, 3, N_GROUPS, LANES)
    dn_o = _delta(proj3, conv_w4, _lane_row(dn_a_log[0], N_GROUPS),
                  _lane_row(dn_dt_bias[0], N_GROUPS), dn_norm_w[0].reshape(1, LANES),
                  batch=batch, seq=seq)

    out = _mlp(h, conv_o, dn_o, w_out[0].astype(BF16), norm2_w[0].reshape(1, d),
               w_mlp_up[0].astype(BF16), w_mlp_down[0].astype(BF16),
               final_norm_w.reshape(1, d))
    return out.reshape(batch, seq, d)
```

```python
import functools

import jax
import jax.numpy as jnp
from jax import lax
from jax.experimental import pallas as pl
from jax.experimental.pallas import tpu as pltpu

F32 = jnp.float32
BF16 = jnp.bfloat16

EPS = 1e-6
LANES = 128
SUBLANES = 8
CONV_WIDTH = 31
SHORT_CONV_WIDTH = 4
N_GROUPS = 8
CONF_HALO = 32
DN_HALO = SUBLANES
CHUNK = 128
N_LEVELS = CHUNK.bit_length() - 1

CB_VAL, CB_GATE, CB_Q, CB_K, CB_V, CB_Z, CB_BA = 0, 8, 16, 24, 32, 40, 48
N_COLBLOCKS = 50

VMEM_LIMIT = 56 * 1024 * 1024


def _sigmoid(x):
    return 0.5 * jnp.tanh(0.5 * x) + 0.5


def _silu(x):
    return x * _sigmoid(x)


def _softplus(x):
    return jnp.maximum(x, 0.0) + jnp.log1p(jnp.exp(-jnp.abs(x)))


def _inproj_kernel(x_ref, nw_ref, w_ref, o_ref, u_scr):
    @pl.when(pl.program_id(1) == 0)
    def _():
        x = x_ref[...]
        ms = jnp.mean(x * x, axis=-1, keepdims=True)
        u_scr[...] = (x * lax.rsqrt(ms + EPS) * nw_ref[...]).astype(BF16)

    res = jnp.dot(u_scr[...], w_ref[...], preferred_element_type=F32)
    for c in range(o_ref.shape[0]):
        o_ref[c] = res[:, c * LANES:(c + 1) * LANES]


def _inproj(x2, norm_w, w_in_p, *, tm=1024, tn=1280):
    t, d = x2.shape
    n = w_in_p.shape[1]
    return pl.pallas_call(
        _inproj_kernel,
        out_shape=jax.ShapeDtypeStruct((n // LANES, t, LANES), F32),
        grid=(t // tm, n // tn),
        in_specs=[
            pl.BlockSpec((tm, d), lambda i, j: (i, 0)),
            pl.BlockSpec((1, d), lambda i, j: (0, 0)),
            pl.BlockSpec((d, tn), lambda i, j: (0, j)),
        ],
        out_specs=pl.BlockSpec((tn // LANES, tm, LANES), lambda i, j: (j, i, 0)),
        scratch_shapes=[pltpu.VMEM((tm, d), BF16)],
        compiler_params=pltpu.CompilerParams(
            dimension_semantics=("arbitrary", "arbitrary"),
            vmem_limit_bytes=VMEM_LIMIT),
        name="inproj",
    )(x2, norm_w, w_in_p)


def _conf_kernel(v_ref, g_ref, bv_ref, bg_ref, w_ref, cb_ref, lg_ref, lb_ref,
                 o_ref, h_scr, y_scr, *, ts, rt):
    i = pl.program_id(2)

    @pl.when(i == 0)
    def _():
        h_scr[0:CONF_HALO, :] = jnp.zeros((CONF_HALO, LANES), F32)

    @pl.when(i > 0)
    def _():
        h_scr[0:CONF_HALO, :] = h_scr[ts:ts + CONF_HALO, :]

    a = v_ref[0] + bv_ref[0]
    g = g_ref[0] + bg_ref[0]
    h_scr[CONF_HALO:CONF_HALO + ts, :] = a * _sigmoid(g)

    first_tap = CONF_HALO - (CONV_WIDTH - 1)

    def body(r, carry):
        base = pl.multiple_of(r * rt, rt)
        acc = w_ref[0, 0:1, :] * h_scr[pl.ds(base + first_tap, rt), :]
        for k in range(1, CONV_WIDTH):
            acc = acc + w_ref[0, k:k + 1, :] * h_scr[pl.ds(base + first_tap + k, rt), :]
        y_scr[pl.ds(base, rt), :] = acc
        return carry

    lax.fori_loop(0, ts // rt, body, 0)

    y = y_scr[...] + cb_ref[0]
    mu = jnp.mean(y, axis=-1, keepdims=True)
    cen = y - mu
    var = jnp.mean(cen * cen, axis=-1, keepdims=True)
    hn = cen * lax.rsqrt(var + EPS) * lg_ref[0] + lb_ref[0]
    o_ref[0] = _silu(hn).astype(o_ref.dtype)


def _conf(proj3, b_glu, dw_w, dw_b, ln_g, ln_b, *, batch, seq, ts=2048, rt=64):
    t = proj3.shape[1]
    ns = seq // ts
    vec = lambda off: pl.BlockSpec((1, 1, LANES), lambda b, c, i: (off + c, 0, 0))
    return pl.pallas_call(
        functools.partial(_conf_kernel, ts=ts, rt=rt),
        out_shape=jax.ShapeDtypeStruct((N_GROUPS, t, LANES), BF16),
        grid=(batch, N_GROUPS, ns),
        in_specs=[
            pl.BlockSpec((1, ts, LANES), lambda b, c, i: (CB_VAL + c, b * ns + i, 0)),
            pl.BlockSpec((1, ts, LANES), lambda b, c, i: (CB_GATE + c, b * ns + i, 0)),
            vec(0), vec(N_GROUPS),
            pl.BlockSpec((1, CONV_WIDTH, LANES), lambda b, c, i: (c, 0, 0)),
            vec(0), vec(0), vec(0),
        ],
        out_specs=pl.BlockSpec((1, ts, LANES), lambda b, c, i: (c, b * ns + i, 0)),
        scratch_shapes=[pltpu.VMEM((CONF_HALO + ts, LANES), F32),
                        pltpu.VMEM((ts, LANES), F32)],
        compiler_params=pltpu.CompilerParams(
            dimension_semantics=("arbitrary", "arbitrary", "arbitrary"),
            vmem_limit_bytes=VMEM_LIMIT),
        name="conf",
    )(proj3, proj3, b_glu, b_glu, dw_w, dw_b, ln_g, ln_b)


def _delta_kernel(q_ref, k_ref, v_ref, z_ref, ba_ref, cw_ref, alog_ref, dtb_ref, nw_ref,
                  o_ref,
                  x_scr, halo_scr, q_scr, k_scr, v_scr,
                  beta_scr, cs_scr, f_scr, ld_scr, mask_scr,
                  u_scr, wq_scr, kgt_scr, intra_scr, o_scr, s_scr, *, ts, group, rt):
    i = pl.program_id(1)
    h = pl.program_id(2)
    nc = ts // CHUNK

    r_io = lax.broadcasted_iota(jnp.int32, (CHUNK, CHUNK), 0)
    c_io = lax.broadcasted_iota(jnp.int32, (CHUNK, CHUNK), 1)
    causal = r_io >= c_io
    strict = r_io > c_io
    eye = jnp.where(r_io == c_io, 1.0, 0.0).astype(F32)

    @pl.when(h == 0)
    def _():
        upper = jnp.where(r_io <= c_io, 1.0, 0.0).astype(F32)
        ones = jnp.ones((CHUNK, CHUNK), F32)
        for lvl in range(N_LEVELS):
            same = (r_io >> (lvl + 1)) == (c_io >> (lvl + 1))
            lower_left = jnp.where(((r_io >> lvl) & 1) == 1, 1.0, 0.0) * \
                jnp.where(((c_io >> lvl) & 1) == 0, 1.0, 0.0)
            mask_scr[lvl] = jnp.where(same, lower_left, 0.0).astype(F32)
        ba_t = ba_ref[0].T
        beta = _sigmoid(ba_t[0:N_GROUPS, :])
        a_raw = ba_t[N_GROUPS:2 * N_GROUPS, :]
        neg_a = -jnp.exp(alog_ref[...])
        for c in range(nc):
            cols = slice(c * CHUNK, (c + 1) * CHUNK)
            g_c = neg_a * _softplus(a_raw[:, cols] + dtb_ref[...])
            cs = jnp.dot(g_c, upper, precision=lax.Precision.HIGHEST,
                         preferred_element_type=F32)
            tot = jnp.dot(g_c, ones, precision=lax.Precision.HIGHEST,
                          preferred_element_type=F32)
            f_c = jnp.exp(tot - cs) * beta[:, cols]
            ld_c = jnp.exp(tot)
            for hh in range(N_GROUPS):
                one = slice(hh, hh + 1)
                beta_scr[hh, :, cols] = beta[one, cols]
                cs_scr[hh, :, cols] = cs[one, :]
                f_scr[hh, :, cols] = f_c[one, :]
                ld_scr[hh, :, cols] = ld_c[one, :]

    @pl.when(i == 0)
    def _():
        halo_scr[h] = jnp.zeros((3, DN_HALO, LANES), F32)
        s_scr[h] = jnp.zeros((LANES, LANES), F32)

    first_tap = DN_HALO - (SHORT_CONV_WIDTH - 1)
    for j, ref in enumerate((q_ref, k_ref, v_ref)):
        x_scr[j, 0:DN_HALO, :] = halo_scr[h, j]
        x_scr[j, DN_HALO:DN_HALO + ts, :] = ref[0]
        halo_scr[h, j] = x_scr[j, ts:ts + DN_HALO, :]

    def conv_tile(r, carry):
        base = pl.multiple_of(r * rt, rt)
        ys = []
        for j in range(3):
            y = cw_ref[0, j, pl.ds(h, 1), :] * x_scr[j, pl.ds(base + first_tap, rt), :]
            for m in range(1, SHORT_CONV_WIDTH):
                y = y + cw_ref[m, j, pl.ds(h, 1), :] * x_scr[j, pl.ds(base + first_tap + m, rt), :]
            ys.append(_silu(y))
        rows = pl.ds(base, rt)
        q_scr[rows, :] = ys[0] * (lax.rsqrt(jnp.sum(ys[0] * ys[0], axis=-1, keepdims=True) + EPS)
                                  * (LANES ** -0.5))
        k_scr[rows, :] = ys[1] * lax.rsqrt(jnp.sum(ys[1] * ys[1], axis=-1, keepdims=True) + EPS)
        v_scr[rows, :] = ys[2].astype(BF16)
        return carry

    lax.fori_loop(0, ts // rt, conv_tile, 0)

    def prep_group(first_chunk, slot0):
        items = []
        for g in range(group):
            r0 = pl.multiple_of((first_chunk + g) * CHUNK, CHUNK)
            rows = pl.ds(r0, CHUNK)
            k_c = k_scr[rows, :]
            qs_c = q_scr[rows, :]
            cs_r = jnp.broadcast_to(cs_scr[h, :, rows], (CHUNK, CHUNK))
            beta_r = jnp.broadcast_to(beta_scr[h, :, rows], (CHUNK, CHUNK))
            cs_c = cs_r.T
            db = jnp.exp(jnp.where(causal, cs_c - cs_r, -jnp.inf)) * beta_r
            kq = lax.dot_general(
                jnp.concatenate([k_c, qs_c], axis=0).astype(BF16), k_c.astype(BF16),
                (((1,), (1,)), ((), ())), preferred_element_type=F32)
            a_mat = jnp.where(strict, kq[:CHUNK] * db, 0.0)
            intra_scr[slot0 + g] = (kq[CHUNK:] * db).astype(BF16)
            eg = jnp.exp(cs_c)
            wq_scr[slot0 + g, CHUNK:2 * CHUNK, :] = (qs_c * eg).astype(BF16)
            kgt_scr[slot0 + g] = (k_c.T * f_scr[h, :, rows]).astype(BF16)
            rhs = jnp.concatenate([v_scr[rows, :], (k_c * eg).astype(BF16)], axis=1)
            items.append(dict(a=a_mat, x=eye - a_mat * mask_scr[0], rhs=rhs))
        yield
        for lvl in range(1, N_LEVELS):
            for it in items:
                it["xb"] = it["x"].astype(BF16)
                it["mx"] = jnp.dot((it["a"] * mask_scr[lvl]).astype(BF16), it["xb"],
                                   preferred_element_type=F32)
            yield
            for it in items:
                it["x"] = it["x"] - jnp.dot(it["xb"], it["mx"].astype(BF16),
                                            preferred_element_type=F32)
            yield
        for g, it in enumerate(items):
            uw = jnp.dot(it["x"].astype(BF16), it["rhs"], preferred_element_type=F32)
            u_scr[slot0 + g] = uw[:, :LANES]
            wq_scr[slot0 + g, 0:CHUNK, :] = uw[:, LANES:].astype(BF16)
        yield

    def step_group(first_chunk, slot0, state_box):
        state = state_box[0]
        for g in range(group):
            r0 = pl.multiple_of((first_chunk + g) * CHUNK, CHUNK)
            rows = pl.ds(r0, CHUNK)
            ws = jnp.dot(wq_scr[slot0 + g], state.astype(BF16), preferred_element_type=F32)
            yield
            v_new = (u_scr[slot0 + g] - ws[:CHUNK]).astype(BF16)
            o_scr[rows, :] = ws[CHUNK:] + jnp.dot(intra_scr[slot0 + g], v_new,
                                                  preferred_element_type=F32)
            state = state * ld_scr[h, :, rows] + jnp.dot(kgt_scr[slot0 + g], v_new,
                                                         preferred_element_type=F32)
            yield
        state_box[0] = state

    def interleave(*gens):
        live = list(gens)
        while live:
            for gen in list(live):
                try:
                    next(gen)
                except StopIteration:
                    live.remove(gen)

    n_groups = nc // group
    interleave(prep_group(0, 0))

    def pair(p, state):
        box = [state]
        g0 = 2 * p * group
        interleave(step_group(g0, 0, box), prep_group(g0 + group, group))
        nxt = jnp.minimum(g0 + 2 * group, (n_groups - 1) * group)
        interleave(step_group(g0 + group, group, box), prep_group(nxt, 0))
        return box[0]

    s_scr[h] = lax.fori_loop(0, n_groups // 2, pair, s_scr[h])

    def out_tile(r, carry):
        rows = pl.ds(pl.multiple_of(r * rt, rt), rt)
        o = o_scr[rows, :]
        o = o * lax.rsqrt(jnp.mean(o * o, axis=-1, keepdims=True) + EPS) * nw_ref[...]
        o_ref[0, rows, :] = (o * _silu(z_ref[0, rows, :])).astype(o_ref.dtype)
        return carry

    lax.fori_loop(0, ts // rt, out_tile, 0)


def _delta(proj3, conv_w4, alog_rep, dtb_rep, norm_row, *, batch, seq, ts=4096, group=8, rt=256):
    t = proj3.shape[1]
    ns = seq // ts
    col = lambda off: pl.BlockSpec((1, ts, LANES), lambda b, i, h: (off + h, b * ns + i, 0))
    rep = pl.BlockSpec((N_GROUPS, LANES), lambda b, i, h: (0, 0))
    return pl.pallas_call(
        functools.partial(_delta_kernel, ts=ts, group=group, rt=rt),
        out_shape=jax.ShapeDtypeStruct((N_GROUPS, t, LANES), BF16),
        grid=(batch, ns, N_GROUPS),
        in_specs=[
            col(CB_Q), col(CB_K), col(CB_V), col(CB_Z),
            pl.BlockSpec((1, ts, LANES), lambda b, i, h: (CB_BA, b * ns + i, 0)),
            pl.BlockSpec((SHORT_CONV_WIDTH, 3, N_GROUPS, LANES), lambda b, i, h: (0, 0, 0, 0)),
            rep, rep,
            pl.BlockSpec((1, LANES), lambda b, i, h: (0, 0)),
        ],
        out_specs=pl.BlockSpec((1, ts, LANES), lambda b, i, h: (h, b * ns + i, 0)),
        scratch_shapes=[
            pltpu.VMEM((3, DN_HALO + ts, LANES), F32),
            pltpu.VMEM((N_GROUPS, 3, DN_HALO, LANES), F32),
            pltpu.VMEM((ts, LANES), F32),
            pltpu.VMEM((ts, LANES), F32),
            pltpu.VMEM((ts, LANES), BF16),
            pltpu.VMEM((N_GROUPS, 1, ts), F32),
            pltpu.VMEM((N_GROUPS, 1, ts), F32),
            pltpu.VMEM((N_GROUPS, 1, ts), F32),
            pltpu.VMEM((N_GROUPS, 1, ts), F32),
            pltpu.VMEM((N_LEVELS, CHUNK, CHUNK), F32),
            pltpu.VMEM((2 * group, CHUNK, LANES), F32),
            pltpu.VMEM((2 * group, 2 * CHUNK, LANES), BF16),
            pltpu.VMEM((2 * group, LANES, CHUNK), BF16),
            pltpu.VMEM((2 * group, CHUNK, CHUNK), BF16),
            pltpu.VMEM((ts, LANES), F32),
            pltpu.VMEM((N_GROUPS, LANES, LANES), F32),
        ],
        compiler_params=pltpu.CompilerParams(
            dimension_semantics=("arbitrary", "arbitrary", "arbitrary"),
            vmem_limit_bytes=VMEM_LIMIT),
        name="delta",
    )(proj3, proj3, proj3, proj3, proj3, conv_w4, alog_rep, dtb_rep, norm_row)


def _mlp_kernel(x_ref, conv_ref, dn_ref, wout_ref, n2_ref, wup_ref, wdn_ref, fn_ref,
                o_ref, r_scr):
    f = pl.program_id(1)

    @pl.when(f == 0)
    def _():
        mix = jnp.concatenate([conv_ref[c] for c in range(N_GROUPS)]
                              + [dn_ref[c] for c in range(N_GROUPS)], axis=-1)
        h1 = x_ref[...] + jnp.dot(mix, wout_ref[...], preferred_element_type=F32)
        o_ref[...] = h1
        ms = jnp.mean(h1 * h1, axis=-1, keepdims=True)
        r_scr[...] = (h1 * lax.rsqrt(ms + EPS) * n2_ref[...]).astype(BF16)

    m = jnp.dot(r_scr[...], wup_ref[...], preferred_element_type=F32)
    act = jnp.square(jnp.maximum(m, 0.0)).astype(BF16)
    o_ref[...] += jnp.dot(act, wdn_ref[...], preferred_element_type=F32)

    @pl.when(f == pl.num_programs(1) - 1)
    def _():
        h2 = o_ref[...]
        ms = jnp.mean(h2 * h2, axis=-1, keepdims=True)
        o_ref[...] = h2 * lax.rsqrt(ms + EPS) * fn_ref[...]


def _mlp(x2, conv_o, dn_o, w_out_b, n2, w_up_b, w_dn_b, fn, *, tm=512, tf=1024):
    t, d = x2.shape
    dff = w_up_b.shape[1]
    return pl.pallas_call(
        _mlp_kernel,
        out_shape=jax.ShapeDtypeStruct((t, d), F32),
        grid=(t // tm, dff // tf),
        in_specs=[
            pl.BlockSpec((tm, d), lambda i, f: (i, 0)),
            pl.BlockSpec((N_GROUPS, tm, LANES), lambda i, f: (0, i, 0)),
            pl.BlockSpec((N_GROUPS, tm, LANES), lambda i, f: (0, i, 0)),
            pl.BlockSpec((d, d), lambda i, f: (0, 0), pipeline_mode=pl.Buffered(1)),
            pl.BlockSpec((1, d), lambda i, f: (0, 0)),
            pl.BlockSpec((d, tf), lambda i, f: (0, f)),
            pl.BlockSpec((tf, d), lambda i, f: (f, 0)),
            pl.BlockSpec((1, d), lambda i, f: (0, 0)),
        ],
        out_specs=pl.BlockSpec((tm, d), lambda i, f: (i, 0)),
        scratch_shapes=[pltpu.VMEM((tm, d), BF16)],
        compiler_params=pltpu.CompilerParams(
            dimension_semantics=("arbitrary", "arbitrary"),
            vmem_limit_bytes=VMEM_LIMIT),
        name="mlp",
    )(x2, conv_o, dn_o, w_out_b, n2, w_up_b, w_dn_b, fn)


def _lane_rep(vals):
    return jnp.broadcast_to(vals.astype(F32)[:, None], (vals.shape[0], LANES))


def kernel(x, norm1_w, w_in, b_glu, conf_dw_w, conf_dw_b, conf_ln_g, conf_ln_b, dn_conv_w,
           dn_a_log, dn_dt_bias, dn_norm_w, w_out, norm2_w, w_mlp_up, w_mlp_down, final_norm_w):
    batch, seq, d = x.shape
    assert norm1_w.shape[0] == 1, "single-layer block"
    h = x.reshape(batch * seq, d)

    d_in = w_in.shape[2]
    w_in_p = jnp.pad(w_in[0], ((0, 0), (0, N_COLBLOCKS * LANES - d_in))).astype(BF16)
    proj3 = _inproj(h, norm1_w[0].reshape(1, d), w_in_p)

    grp = lambda v: v.reshape(-1, 1, LANES)
    dw_w = conf_dw_w[0].reshape(CONV_WIDTH, N_GROUPS, LANES).transpose(1, 0, 2)
    conv_o = _conf(proj3, grp(b_glu[0]), dw_w, grp(conf_dw_b[0]), grp(conf_ln_g[0]),
                   grp(conf_ln_b[0]), batch=batch, seq=seq)

    conv_w4 = dn_conv_w[0].reshape(SHORT_CONV_WIDTH, 3, N_GROUPS, LANES)
    dn_o = _delta(proj3, conv_w4, _lane_rep(dn_a_log[0]), _lane_rep(dn_dt_bias[0]),
                  dn_norm_w[0].reshape(1, LANES), batch=batch, seq=seq)

    out = _mlp(h, conv_o, dn_o, w_out[0].astype(BF16), norm2_w[0].reshape(1, d),
               w_mlp_up[0].astype(BF16), w_mlp_down[0].astype(BF16),
               final_norm_w.reshape(1, d))
    return out.reshape(batch, seq, d)
```

```python
import functools

import jax
import jax.numpy as jnp
from jax import lax
from jax.experimental import pallas as pl
from jax.experimental.pallas import tpu as pltpu

F32 = jnp.float32
BF16 = jnp.bfloat16

EPS = 1e-6
LANES = 128
SUBLANES = 8
CONV_WIDTH = 31
SHORT_CONV_WIDTH = 4
N_GROUPS = 8
CONF_HALO = 32
DN_HALO = SUBLANES
CHUNK = 128
N_LEVELS = CHUNK.bit_length() - 1

CB_VAL, CB_GATE, CB_Q, CB_K, CB_V, CB_Z, CB_BA = 0, 8, 16, 24, 32, 40, 48
N_COLBLOCKS = 50

VMEM_LIMIT = 56 * 1024 * 1024


def _sigmoid(x):
    return 0.5 * jnp.tanh(0.5 * x) + 0.5


def _silu(x):
    return x * _sigmoid(x)


def _derived_zero(x):
    bits = lax.bitcast_convert_type(x, jnp.uint32)
    bits = lax.shift_right_logical(lax.shift_right_logical(bits, jnp.uint32(16)), jnp.uint32(16))
    return lax.bitcast_convert_type(bits, F32)


def _softplus(x):
    return jnp.maximum(x, 0.0) + jnp.log1p(jnp.exp(-jnp.abs(x)))


def _inproj_kernel(x_ref, nw_ref, w_ref, o_ref, u_scr):
    @pl.when(pl.program_id(1) == 0)
    def _():
        x = x_ref[...]
        ms = jnp.mean(x * x, axis=-1, keepdims=True)
        u_scr[...] = (x * lax.rsqrt(ms + EPS) * nw_ref[...]).astype(BF16)

    res = jnp.dot(u_scr[...], w_ref[...], preferred_element_type=F32)
    for c in range(o_ref.shape[0]):
        o_ref[c] = res[:, c * LANES:(c + 1) * LANES]


def _inproj(x2, norm_w, w_in_p, *, tm=1024, tn=1280):
    t, d = x2.shape
    n = w_in_p.shape[1]
    return pl.pallas_call(
        _inproj_kernel,
        out_shape=jax.ShapeDtypeStruct((n // LANES, t, LANES), F32),
        grid=(t // tm, n // tn),
        in_specs=[
            pl.BlockSpec((tm, d), lambda i, j: (i, 0)),
            pl.BlockSpec((1, d), lambda i, j: (0, 0)),
            pl.BlockSpec((d, tn), lambda i, j: (0, j)),
        ],
        out_specs=pl.BlockSpec((tn // LANES, tm, LANES), lambda i, j: (j, i, 0)),
        scratch_shapes=[pltpu.VMEM((tm, d), BF16)],
        compiler_params=pltpu.CompilerParams(
            dimension_semantics=("arbitrary", "arbitrary"),
            vmem_limit_bytes=VMEM_LIMIT),
        name="inproj",
    )(x2, norm_w, w_in_p)


def _delta_kernel(q_ref, k_ref, v_ref, z_ref, ba_ref, cw_ref, alog_ref, dtb_ref, nw_ref,
                  o_ref,
                  x_scr, halo_scr, q_scr, k_scr, v_scr,
                  beta_scr, cs_scr, f_scr, ld_scr, mask_scr,
                  u_scr, wq_scr, kgt_scr, intra_scr, o_scr, s_scr, *, ts, group, rt):
    i = pl.program_id(1)
    h = pl.program_id(2)
    nc = ts // CHUNK

    r_io = lax.broadcasted_iota(jnp.int32, (CHUNK, CHUNK), 0)
    c_io = lax.broadcasted_iota(jnp.int32, (CHUNK, CHUNK), 1)
    causal = r_io >= c_io
    strict = r_io > c_io
    eye = jnp.where(r_io == c_io, 1.0, 0.0).astype(F32)

    @pl.when(h == 0)
    def _():
        upper = jnp.where(r_io <= c_io, 1.0, 0.0).astype(F32)
        ones = jnp.ones((CHUNK, CHUNK), F32)
        for lvl in range(N_LEVELS):
            same = (r_io >> (lvl + 1)) == (c_io >> (lvl + 1))
            lower_left = jnp.where(((r_io >> lvl) & 1) == 1, 1.0, 0.0) * \
                jnp.where(((c_io >> lvl) & 1) == 0, 1.0, 0.0)
            mask_scr[lvl] = jnp.where(same, lower_left, 0.0).astype(F32)
        ba_t = ba_ref[0].T
        beta = _sigmoid(ba_t[0:N_GROUPS, :])
        a_raw = ba_t[N_GROUPS:2 * N_GROUPS, :]
        neg_a = -jnp.exp(alog_ref[...])
        for c in range(nc):
            cols = slice(c * CHUNK, (c + 1) * CHUNK)
            g_c = neg_a * _softplus(a_raw[:, cols] + dtb_ref[...])
            cs = jnp.dot(g_c, upper, precision=lax.Precision.HIGHEST,
                         preferred_element_type=F32)
            tot = jnp.dot(g_c, ones, precision=lax.Precision.HIGHEST,
                          preferred_element_type=F32)
            f_c = jnp.exp(tot - cs) * beta[:, cols]
            ld_c = jnp.exp(tot)
            for hh in range(N_GROUPS):
                one = slice(hh, hh + 1)
                beta_scr[hh, :, cols] = beta[one, cols]
                cs_scr[hh, :, cols] = cs[one, :]
                f_scr[hh, :, cols] = f_c[one, :]
                ld_scr[hh, :, cols] = ld_c[one, :]

    @pl.when(i == 0)
    def _():
        halo_scr[h] = jnp.zeros((3, DN_HALO, LANES), F32)
        s_scr[h] = jnp.zeros((LANES, LANES), F32)

    first_tap = DN_HALO - (SHORT_CONV_WIDTH - 1)
    for j, ref in enumerate((q_ref, k_ref, v_ref)):
        x_scr[j, 0:DN_HALO, :] = halo_scr[h, j]
        x_scr[j, DN_HALO:DN_HALO + ts, :] = ref[0]
        halo_scr[h, j] = x_scr[j, ts:ts + DN_HALO, :]

    def conv_tile(r, carry):
        base = pl.multiple_of(r * rt, rt)
        ys = []
        for j in range(3):
            y = cw_ref[0, j, pl.ds(h, 1), :] * x_scr[j, pl.ds(base + first_tap, rt), :]
            for m in range(1, SHORT_CONV_WIDTH):
                y = y + cw_ref[m, j, pl.ds(h, 1), :] * x_scr[j, pl.ds(base + first_tap + m, rt), :]
            ys.append(_silu(y))
        rows = pl.ds(base, rt)
        q_scr[rows, :] = ys[0] * (lax.rsqrt(jnp.sum(ys[0] * ys[0], axis=-1, keepdims=True) + EPS)
                                  * (LANES ** -0.5))
        k_scr[rows, :] = ys[1] * lax.rsqrt(jnp.sum(ys[1] * ys[1], axis=-1, keepdims=True) + EPS)
        v_scr[rows, :] = ys[2].astype(BF16)
        return carry

    lax.fori_loop(0, ts // rt, conv_tile, 0)

    def prep_group(first_chunk, slot0):
        items = []
        for g in range(group):
            r0 = pl.multiple_of((first_chunk + g) * CHUNK, CHUNK)
            rows = pl.ds(r0, CHUNK)
            k_c = k_scr[rows, :]
            qs_c = q_scr[rows, :]
            cs_r = jnp.broadcast_to(cs_scr[h, :, rows], (CHUNK, CHUNK))
            beta_r = jnp.broadcast_to(beta_scr[h, :, rows], (CHUNK, CHUNK))
            cs_c = cs_r.T
            db = jnp.exp(jnp.where(causal, cs_c - cs_r, -jnp.inf)) * beta_r
            kq = lax.dot_general(
                jnp.concatenate([k_c, qs_c], axis=0).astype(BF16), k_c.astype(BF16),
                (((1,), (1,)), ((), ())), preferred_element_type=F32)
            a_mat = jnp.where(strict, kq[:CHUNK] * db, 0.0)
            intra_scr[slot0 + g] = (kq[CHUNK:] * db).astype(BF16)
            eg = jnp.exp(cs_c)
            wq_scr[slot0 + g, CHUNK:2 * CHUNK, :] = (qs_c * eg).astype(BF16)
            kgt_scr[slot0 + g] = (k_c.T * f_scr[h, :, rows]).astype(BF16)
            rhs = jnp.concatenate([v_scr[rows, :], (k_c * eg).astype(BF16)], axis=1)
            items.append(dict(a=a_mat, x=eye - a_mat * mask_scr[0], rhs=rhs))
        yield
        for lvl in range(1, N_LEVELS):
            for it in items:
                it["xb"] = it["x"].astype(BF16)
                it["mx"] = jnp.dot((it["a"] * mask_scr[lvl]).astype(BF16), it["xb"],
                                   preferred_element_type=F32)
            yield
            for it in items:
                it["x"] = it["x"] - jnp.dot(it["xb"], it["mx"].astype(BF16),
                                            preferred_element_type=F32)
            yield
        for g, it in enumerate(items):
            uw = jnp.dot(it["x"].astype(BF16), it["rhs"], preferred_element_type=F32)
            u_scr[slot0 + g] = uw[:, :LANES]
            wq_scr[slot0 + g, 0:CHUNK, :] = uw[:, LANES:].astype(BF16)
        yield

    def step_group(first_chunk, slot0, state_box):
        state = state_box[0]
        for g in range(group):
            r0 = pl.multiple_of((first_chunk + g) * CHUNK, CHUNK)
            rows = pl.ds(r0, CHUNK)
            ws = jnp.dot(wq_scr[slot0 + g], state.astype(BF16), preferred_element_type=F32)
            yield
            v_new = (u_scr[slot0 + g] - ws[:CHUNK]).astype(BF16)
            o_scr[rows, :] = ws[CHUNK:] + jnp.dot(intra_scr[slot0 + g], v_new,
                                                  preferred_element_type=F32)
            state = state * ld_scr[h, :, rows] + jnp.dot(kgt_scr[slot0 + g], v_new,
                                                         preferred_element_type=F32)
            yield
        state_box[0] = state

    def interleave(*gens):
        live = list(gens)
        while live:
            for gen in list(live):
                try:
                    next(gen)
                except StopIteration:
                    live.remove(gen)

    n_groups = nc // group
    interleave(prep_group(0, 0))

    def pair(p, state):
        box = [state]
        g0 = 2 * p * group
        interleave(step_group(g0, 0, box), prep_group(g0 + group, group))
        nxt = jnp.minimum(g0 + 2 * group, (n_groups - 1) * group)
        interleave(step_group(g0 + group, group, box), prep_group(nxt, 0))
        return box[0]

    s_scr[h] = lax.fori_loop(0, n_groups // 2, pair, s_scr[h])

    def out_tile(r, carry):
        rows = pl.ds(pl.multiple_of(r * rt, rt), rt)
        o = o_scr[rows, :]
        o = o * lax.rsqrt(jnp.mean(o * o, axis=-1, keepdims=True) + EPS) * nw_ref[...]
        o_ref[0, rows, :] = (o * _silu(z_ref[0, rows, :])).astype(o_ref.dtype)
        return carry

    lax.fori_loop(0, ts // rt, out_tile, 0, unroll=2)


def _delta(proj3, conv_w4, alog_rep, dtb_rep, norm_row, *, batch, seq, ts=4096, group=8, rt=256):
    t = proj3.shape[1]
    ns = seq // ts
    col = lambda off: pl.BlockSpec((1, ts, LANES), lambda b, i, h: (off + h, b * ns + i, 0))
    rep = pl.BlockSpec((N_GROUPS, LANES), lambda b, i, h: (0, 0))
    return pl.pallas_call(
        functools.partial(_delta_kernel, ts=ts, group=group, rt=rt),
        out_shape=jax.ShapeDtypeStruct((N_GROUPS, t, LANES), BF16),
        grid=(batch, ns, N_GROUPS),
        in_specs=[
            col(CB_Q), col(CB_K), col(CB_V), col(CB_Z),
            pl.BlockSpec((1, ts, LANES), lambda b, i, h: (CB_BA, b * ns + i, 0)),
            pl.BlockSpec((SHORT_CONV_WIDTH, 3, N_GROUPS, LANES), lambda b, i, h: (0, 0, 0, 0)),
            rep, rep,
            pl.BlockSpec((1, LANES), lambda b, i, h: (0, 0)),
        ],
        out_specs=pl.BlockSpec((1, ts, LANES), lambda b, i, h: (h, b * ns + i, 0)),
        scratch_shapes=[
            pltpu.VMEM((3, DN_HALO + ts, LANES), F32),
            pltpu.VMEM((N_GROUPS, 3, DN_HALO, LANES), F32),
            pltpu.VMEM((ts, LANES), F32),
            pltpu.VMEM((ts, LANES), F32),
            pltpu.VMEM((ts, LANES), BF16),
            pltpu.VMEM((N_GROUPS, 1, ts), F32),
            pltpu.VMEM((N_GROUPS, 1, ts), F32),
            pltpu.VMEM((N_GROUPS, 1, ts), F32),
            pltpu.VMEM((N_GROUPS, 1, ts), F32),
            pltpu.VMEM((N_LEVELS, CHUNK, CHUNK), F32),
            pltpu.VMEM((2 * group, CHUNK, LANES), F32),
            pltpu.VMEM((2 * group, 2 * CHUNK, LANES), BF16),
            pltpu.VMEM((2 * group, LANES, CHUNK), BF16),
            pltpu.VMEM((2 * group, CHUNK, CHUNK), BF16),
            pltpu.VMEM((ts, LANES), F32),
            pltpu.VMEM((N_GROUPS, LANES, LANES), F32),
        ],
        compiler_params=pltpu.CompilerParams(
            dimension_semantics=("arbitrary", "arbitrary", "arbitrary"),
            vmem_limit_bytes=VMEM_LIMIT),
        name="delta",
    )(proj3, proj3, proj3, proj3, proj3, conv_w4, alog_rep, dtb_rep, norm_row)


def _mlp_kernel(x_ref, val_ref, gate_ref, bv_ref, bg_ref, dw_ref, cb_ref, lg_ref, lb_ref,
                dn_ref, wout_ref, n2_ref, wup_ref, wdn_ref, fn_ref,
                o_ref, r_scr, h_scr, halo_scr, conv_scr, *, tm, rt, blocks_per_seq):
    i = pl.program_id(0)
    f = pl.program_id(1)
    n_blocks = pl.num_programs(0) - 1
    first_tap = CONF_HALO - (CONV_WIDTH - 1)

    def conf_part():
        seq_start = (i % blocks_per_seq) == 0
        h_scr[0:CONF_HALO, :] = jnp.where(seq_start, 0.0, halo_scr[f])
        a = val_ref[0] + bv_ref[0]
        g = gate_ref[0] + bg_ref[0]
        glu = a * _sigmoid(g)
        h_scr[CONF_HALO:CONF_HALO + tm, :] = glu
        zero_glu = _derived_zero(jnp.max(glu.reshape(tm // SUBLANES, SUBLANES, LANES), axis=0))
        halo_scr[f] = h_scr[tm:tm + CONF_HALO, :]
        slot = i % 2
        zeros = []
        for r in range(tm // rt):
            base = r * rt
            acc = dw_ref[0, 0:1, :] * h_scr[base + first_tap:base + first_tap + rt, :]
            for k in range(1, CONV_WIDTH):
                lo = base + first_tap + k
                acc = acc + dw_ref[0, k:k + 1, :] * h_scr[lo:lo + rt, :]
            y = acc + cb_ref[0]
            mu = jnp.mean(y, axis=-1, keepdims=True)
            cen = y - mu
            var = jnp.mean(cen * cen, axis=-1, keepdims=True)
            hn = cen * lax.rsqrt(var + EPS) * lg_ref[0] + lb_ref[0]
            out = _silu(hn)
            conv_scr[slot, f, base:base + rt, :] = out.astype(BF16)
            zeros.append(_derived_zero(out))
        return zero_glu, zeros

    def mlp_head():
        @pl.when(f == 0)
        def _():
            slot = (i + 1) % 2
            mix = jnp.concatenate([conv_scr[slot, c] for c in range(N_GROUPS)]
                                  + [dn_ref[c] for c in range(N_GROUPS)], axis=-1)
            h1 = x_ref[...] + jnp.dot(mix, wout_ref[...], preferred_element_type=F32)
            o_ref[...] = h1
            ms = jnp.mean(h1 * h1, axis=-1, keepdims=True)
            r_scr[...] = (h1 * lax.rsqrt(ms + EPS) * n2_ref[...]).astype(BF16)

    def mlp_body(zero_glu=None, zeros=None):
        m = jnp.dot(r_scr[...], wup_ref[...], preferred_element_type=F32)
        if zeros is not None:
            half = m.shape[1] // 2
            m = jnp.concatenate([
                jnp.concatenate([m[0:SUBLANES, 0:LANES] + zero_glu, m[SUBLANES:, 0:LANES]], axis=0),
                m[:, LANES:half],
                jnp.concatenate([m[0:rt, half:half + LANES] + zeros[0], m[rt:, half:half + LANES]],
                                axis=0),
                m[:, half + LANES:]], axis=1)
        act = jnp.square(jnp.maximum(m, 0.0)).astype(BF16)
        res = jnp.dot(act, wdn_ref[...], preferred_element_type=F32)
        if zeros is None:
            o_ref[...] += res
            return
        slab = o_ref.shape[1] // len(zeros)
        o_ref[:, 0:slab] += res[:, 0:slab]
        for n in range(1, len(zeros)):
            c0, r0 = n * slab, n * rt
            o_ref[:, c0 + LANES:c0 + slab] += res[:, c0 + LANES:c0 + slab]
            o_ref[0:r0, c0:c0 + LANES] += res[0:r0, c0:c0 + LANES]
            o_ref[r0:r0 + rt, c0:c0 + LANES] += res[r0:r0 + rt, c0:c0 + LANES] + zeros[n]
            if r0 + rt < tm:
                o_ref[r0 + rt:tm, c0:c0 + LANES] += res[r0 + rt:tm, c0:c0 + LANES]

    def mlp_tail():
        @pl.when(f == pl.num_programs(1) - 1)
        def _():
            h2 = o_ref[...]
            ms = jnp.mean(h2 * h2, axis=-1, keepdims=True)
            o_ref[...] = h2 * lax.rsqrt(ms + EPS) * fn_ref[...]

    @pl.when(i == 0)
    def _():
        @pl.when(f == 0)
        def _():
            halo_scr[...] = jnp.zeros_like(halo_scr)
        conf_part()

    @pl.when(jnp.logical_and(i > 0, i < n_blocks))
    def _():
        mlp_head()
        mlp_body(*conf_part())
        mlp_tail()

    @pl.when(i == n_blocks)
    def _():
        mlp_head()
        mlp_body()
        mlp_tail()


def _mlp(x2, proj3, b_glu, dw_w, dw_b, ln_g, ln_b, dn_o, w_out_b, n2, w_up_b, w_dn_b, fn,
         *, seq, tm=512, rt=64):
    t, d = x2.shape
    dff = w_up_b.shape[1]
    nb = t // tm
    nf = N_GROUPS
    tf = dff // nf
    prev = lambda i: jnp.maximum(i - 1, 0)
    nxt = lambda i: jnp.minimum(i, nb - 1)
    vec = lambda off: pl.BlockSpec((1, 1, LANES), lambda i, f: (off + f, 0, 0))
    return pl.pallas_call(
        functools.partial(_mlp_kernel, tm=tm, rt=rt, blocks_per_seq=seq // tm),
        out_shape=jax.ShapeDtypeStruct((t, d), F32),
        grid=(nb + 1, nf),
        in_specs=[
            pl.BlockSpec((tm, d), lambda i, f: (prev(i), 0)),
            pl.BlockSpec((1, tm, LANES), lambda i, f: (CB_VAL + f, nxt(i), 0)),
            pl.BlockSpec((1, tm, LANES), lambda i, f: (CB_GATE + f, nxt(i), 0)),
            vec(0), vec(N_GROUPS),
            pl.BlockSpec((1, CONV_WIDTH, LANES), lambda i, f: (f, 0, 0)),
            vec(0), vec(0), vec(0),
            pl.BlockSpec((N_GROUPS, tm, LANES), lambda i, f: (0, prev(i), 0)),
            pl.BlockSpec((d, d), lambda i, f: (0, 0), pipeline_mode=pl.Buffered(1)),
            pl.BlockSpec((1, d), lambda i, f: (0, 0)),
            pl.BlockSpec((d, tf), lambda i, f: (0, f)),
            pl.BlockSpec((tf, d), lambda i, f: (f, 0)),
            pl.BlockSpec((1, d), lambda i, f: (0, 0)),
        ],
        out_specs=pl.BlockSpec((tm, d), lambda i, f: (prev(i), 0)),
        scratch_shapes=[
            pltpu.VMEM((tm, d), BF16),
            pltpu.VMEM((CONF_HALO + tm, LANES), F32),
            pltpu.VMEM((N_GROUPS, CONF_HALO, LANES), F32),
            pltpu.VMEM((2, N_GROUPS, tm, LANES), BF16),
        ],
        compiler_params=pltpu.CompilerParams(
            dimension_semantics=("arbitrary", "arbitrary"),
            vmem_limit_bytes=VMEM_LIMIT),
        name="mlp",
    )(x2, proj3, proj3, b_glu, b_glu, dw_w, dw_b, ln_g, ln_b, dn_o,
      w_out_b, n2, w_up_b, w_dn_b, fn)


def _lane_rep(vals):
    return jnp.broadcast_to(vals.astype(F32)[:, None], (vals.shape[0], LANES))


def kernel(x, norm1_w, w_in, b_glu, conf_dw_w, conf_dw_b, conf_ln_g, conf_ln_b, dn_conv_w,
           dn_a_log, dn_dt_bias, dn_norm_w, w_out, norm2_w, w_mlp_up, w_mlp_down, final_norm_w):
    batch, seq, d = x.shape
    assert norm1_w.shape[0] == 1, "single-layer block"
    h = x.reshape(batch * seq, d)

    d_in = w_in.shape[2]
    w_in_p = jnp.pad(w_in[0], ((0, 0), (0, N_COLBLOCKS * LANES - d_in))).astype(BF16)
    proj3 = _inproj(h, norm1_w[0].reshape(1, d), w_in_p)

    conv_w4 = dn_conv_w[0].reshape(SHORT_CONV_WIDTH, 3, N_GROUPS, LANES)
    dn_o = _delta(proj3, conv_w4, _lane_rep(dn_a_log[0]), _lane_rep(dn_dt_bias[0]),
                  dn_norm_w[0].reshape(1, LANES), batch=batch, seq=seq)

    grp = lambda v: v.reshape(-1, 1, LANES)
    dw_w = conf_dw_w[0].reshape(CONV_WIDTH, N_GROUPS, LANES).transpose(1, 0, 2)
    out = _mlp(h, proj3, grp(b_glu[0]), dw_w, grp(conf_dw_b[0]), grp(conf_ln_g[0]),
               grp(conf_ln_b[0]), dn_o, w_out[0].astype(BF16), norm2_w[0].reshape(1, d),
               w_mlp_up[0].astype(BF16), w_mlp_down[0].astype(BF16),
               final_norm_w.reshape(1, d), seq=seq)
    return out.reshape(batch, seq, d)
```

```python
import functools

import jax
import jax.numpy as jnp
from jax import lax
from jax.experimental import pallas as pl
from jax.experimental.pallas import tpu as pltpu

F32 = jnp.float32
BF16 = jnp.bfloat16

EPS = 1e-6
LANES = 128
SUBLANES = 8
CONV_WIDTH = 31
SHORT_CONV_WIDTH = 4
N_GROUPS = 8
CONF_HALO = 32
DN_HALO = SUBLANES
CHUNK = 128
N_LEVELS = CHUNK.bit_length() - 1
HEADS_PER_STEP = 2

CB_VAL, CB_GATE, CB_Q, CB_K, CB_V, CB_Z, CB_BA = 0, 8, 16, 24, 32, 40, 48
N_COLBLOCKS = 50

VMEM_LIMIT = 56 * 1024 * 1024


def _sigmoid(x):
    return 0.5 * jnp.tanh(0.5 * x) + 0.5


def _silu(x):
    return x * _sigmoid(x)


def _derived_zero(x):
    bits = lax.bitcast_convert_type(x, jnp.uint32)
    bits = lax.shift_right_logical(lax.shift_right_logical(bits, jnp.uint32(16)), jnp.uint32(16))
    return lax.bitcast_convert_type(bits, F32)


def _softplus(x):
    return jnp.maximum(x, 0.0) + jnp.log1p(jnp.exp(-jnp.abs(x)))


def _inproj_kernel(x_ref, nw_ref, w_ref, o_ref, u_scr):
    @pl.when(pl.program_id(1) == 0)
    def _():
        x = x_ref[...]
        ms = jnp.mean(x * x, axis=-1, keepdims=True)
        u_scr[...] = (x * lax.rsqrt(ms + EPS) * nw_ref[...]).astype(BF16)

    res = jnp.dot(u_scr[...], w_ref[...], preferred_element_type=F32)
    for c in range(o_ref.shape[0]):
        o_ref[c] = res[:, c * LANES:(c + 1) * LANES]


def _inproj(x2, norm_w, w_in_p, *, tm=1024, tn=1280):
    t, d = x2.shape
    n = w_in_p.shape[1]
    return pl.pallas_call(
        _inproj_kernel,
        out_shape=jax.ShapeDtypeStruct((n // LANES, t, LANES), F32),
        grid=(t // tm, n // tn),
        in_specs=[
            pl.BlockSpec((tm, d), lambda i, j: (i, 0)),
            pl.BlockSpec((1, d), lambda i, j: (0, 0)),
            pl.BlockSpec((d, tn), lambda i, j: (0, j)),
        ],
        out_specs=pl.BlockSpec((tn // LANES, tm, LANES), lambda i, j: (j, i, 0)),
        scratch_shapes=[pltpu.VMEM((tm, d), BF16)],
        compiler_params=pltpu.CompilerParams(
            dimension_semantics=("arbitrary", "arbitrary"),
            vmem_limit_bytes=VMEM_LIMIT),
        name="inproj",
    )(x2, norm_w, w_in_p)


def _delta_kernel(q_ref, k_ref, v_ref, z_ref, ba_ref, cw_ref, alog_ref, dtb_ref, nw_ref,
                  o_ref,
                  x_scr, halo_scr, q_scr, k_scr, v_scr,
                  beta_scr, cs_scr, f_scr, ld_scr, mask_scr,
                  u_scr, wq_scr, kgt_scr, intra_scr, o_scr, s_scr, *, ts, group, rt):
    i = pl.program_id(1)
    hp = pl.program_id(2)
    heads = [HEADS_PER_STEP * hp + hh for hh in range(HEADS_PER_STEP)]
    nc = ts // CHUNK

    r_io = lax.broadcasted_iota(jnp.int32, (CHUNK, CHUNK), 0)
    c_io = lax.broadcasted_iota(jnp.int32, (CHUNK, CHUNK), 1)
    causal = r_io >= c_io
    strict = r_io > c_io
    eye = jnp.where(r_io == c_io, 1.0, 0.0).astype(F32)

    @pl.when(hp == 0)
    def _():
        upper = jnp.where(r_io <= c_io, 1.0, 0.0).astype(F32)
        ones = jnp.ones((CHUNK, CHUNK), F32)
        for lvl in range(N_LEVELS):
            same = (r_io >> (lvl + 1)) == (c_io >> (lvl + 1))
            lower_left = jnp.where(((r_io >> lvl) & 1) == 1, 1.0, 0.0) * \
                jnp.where(((c_io >> lvl) & 1) == 0, 1.0, 0.0)
            mask_scr[lvl] = jnp.where(same, lower_left, 0.0).astype(F32)
        ba_t = ba_ref[0].T
        beta = _sigmoid(ba_t[0:N_GROUPS, :])
        a_raw = ba_t[N_GROUPS:2 * N_GROUPS, :]
        neg_a = -jnp.exp(alog_ref[...])
        for c in range(nc):
            cols = slice(c * CHUNK, (c + 1) * CHUNK)
            g_c = neg_a * _softplus(a_raw[:, cols] + dtb_ref[...])
            cs = jnp.dot(g_c, upper, precision=lax.Precision.HIGHEST,
                         preferred_element_type=F32)
            tot = jnp.dot(g_c, ones, precision=lax.Precision.HIGHEST,
                          preferred_element_type=F32)
            f_c = jnp.exp(tot - cs) * beta[:, cols]
            ld_c = jnp.exp(tot)
            for hh in range(N_GROUPS):
                one = slice(hh, hh + 1)
                beta_scr[hh, :, cols] = beta[one, cols]
                cs_scr[hh, :, cols] = cs[one, :]
                f_scr[hh, :, cols] = f_c[one, :]
                ld_scr[hh, :, cols] = ld_c[one, :]

    @pl.when(i == 0)
    def _():
        for h in heads:
            halo_scr[h] = jnp.zeros((3, DN_HALO, LANES), F32)
            s_scr[h] = jnp.zeros((LANES, LANES), F32)

    first_tap = DN_HALO - (SHORT_CONV_WIDTH - 1)
    for hh, h in enumerate(heads):
        for j, ref in enumerate((q_ref, k_ref, v_ref)):
            x_scr[j, hh, 0:DN_HALO, :] = halo_scr[h, j]
            x_scr[j, hh, DN_HALO:DN_HALO + ts, :] = ref[hh]
            halo_scr[h, j] = x_scr[j, hh, ts:ts + DN_HALO, :]

    def conv_tile(r, carry):
        base = pl.multiple_of(r * rt, rt)
        rows = pl.ds(base, rt)
        for hh, h in enumerate(heads):
            ys = []
            for j in range(3):
                y = cw_ref[0, j, pl.ds(h, 1), :] * x_scr[j, hh, pl.ds(base + first_tap, rt), :]
                for m in range(1, SHORT_CONV_WIDTH):
                    y = y + (cw_ref[m, j, pl.ds(h, 1), :]
                             * x_scr[j, hh, pl.ds(base + first_tap + m, rt), :])
                ys.append(_silu(y))
            q_scr[hh, rows, :] = ys[0] * (
                lax.rsqrt(jnp.sum(ys[0] * ys[0], axis=-1, keepdims=True) + EPS) * (LANES ** -0.5))
            k_scr[hh, rows, :] = ys[1] * lax.rsqrt(
                jnp.sum(ys[1] * ys[1], axis=-1, keepdims=True) + EPS)
            v_scr[hh, rows, :] = ys[2].astype(BF16)
        return carry

    lax.fori_loop(0, ts // rt, conv_tile, 0)

    def prep_group(first_chunk, slot0):
        items = []
        for hh, h in enumerate(heads):
            for g in range(group):
                slot = slot0 + hh * group + g
                r0 = pl.multiple_of((first_chunk + g) * CHUNK, CHUNK)
                rows = pl.ds(r0, CHUNK)
                k_c = k_scr[hh, rows, :]
                qs_c = q_scr[hh, rows, :]
                cs_r = jnp.broadcast_to(cs_scr[h, :, rows], (CHUNK, CHUNK))
                beta_r = jnp.broadcast_to(beta_scr[h, :, rows], (CHUNK, CHUNK))
                cs_c = cs_r.T
                db = jnp.exp(jnp.where(causal, cs_c - cs_r, -jnp.inf)) * beta_r
                kq = lax.dot_general(
                    jnp.concatenate([k_c, qs_c], axis=0).astype(BF16), k_c.astype(BF16),
                    (((1,), (1,)), ((), ())), preferred_element_type=F32)
                a_mat = jnp.where(strict, kq[:CHUNK] * db, 0.0)
                intra_scr[slot] = (kq[CHUNK:] * db).astype(BF16)
                eg = jnp.exp(cs_c)
                wq_scr[slot, CHUNK:2 * CHUNK, :] = (qs_c * eg).astype(BF16)
                kgt_scr[slot] = (k_c.T * f_scr[h, :, rows]).astype(BF16)
                rhs = jnp.concatenate([v_scr[hh, rows, :], (k_c * eg).astype(BF16)], axis=1)
                items.append(dict(a=a_mat, x=eye - a_mat * mask_scr[0], rhs=rhs, slot=slot))
        yield
        for lvl in range(1, N_LEVELS):
            for it in items:
                it["xb"] = it["x"].astype(BF16)
                it["mx"] = jnp.dot((it["a"] * mask_scr[lvl]).astype(BF16), it["xb"],
                                   preferred_element_type=F32)
            yield
            for it in items:
                it["x"] = it["x"] - jnp.dot(it["xb"], it["mx"].astype(BF16),
                                            preferred_element_type=F32)
            yield
        for it in items:
            uw = jnp.dot(it["x"].astype(BF16), it["rhs"], preferred_element_type=F32)
            u_scr[it["slot"]] = uw[:, :LANES]
            wq_scr[it["slot"], 0:CHUNK, :] = uw[:, LANES:].astype(BF16)
        yield

    def step_group(first_chunk, slot0, states):
        for g in range(group):
            r0 = pl.multiple_of((first_chunk + g) * CHUNK, CHUNK)
            rows = pl.ds(r0, CHUNK)
            ws = [jnp.dot(wq_scr[slot0 + hh * group + g], states[hh].astype(BF16),
                          preferred_element_type=F32) for hh in range(HEADS_PER_STEP)]
            yield
            for hh, h in enumerate(heads):
                slot = slot0 + hh * group + g
                v_new = (u_scr[slot] - ws[hh][:CHUNK]).astype(BF16)
                o_scr[hh, rows, :] = ws[hh][CHUNK:] + jnp.dot(intra_scr[slot], v_new,
                                                             preferred_element_type=F32)
                states[hh] = states[hh] * ld_scr[h, :, rows] + jnp.dot(
                    kgt_scr[slot], v_new, preferred_element_type=F32)
            yield

    def interleave(*gens):
        live = list(gens)
        while live:
            for gen in list(live):
                try:
                    next(gen)
                except StopIteration:
                    live.remove(gen)

    n_groups = nc // group
    set_size = HEADS_PER_STEP * group
    interleave(prep_group(0, 0))

    def pair(p, carry):
        states = list(carry)
        g0 = 2 * p * group
        interleave(step_group(g0, 0, states), prep_group(g0 + group, set_size))
        nxt = jnp.minimum(g0 + 2 * group, (n_groups - 1) * group)
        interleave(step_group(g0 + group, set_size, states), prep_group(nxt, 0))
        return tuple(states)

    final = lax.fori_loop(0, n_groups // 2, pair, tuple(s_scr[h] for h in heads))
    for hh, h in enumerate(heads):
        s_scr[h] = final[hh]

    def out_tile(r, carry):
        rows = pl.ds(pl.multiple_of(r * rt, rt), rt)
        for hh in range(HEADS_PER_STEP):
            o = o_scr[hh, rows, :]
            o = o * lax.rsqrt(jnp.mean(o * o, axis=-1, keepdims=True) + EPS) * nw_ref[...]
            o_ref[hh, rows, :] = (o * _silu(z_ref[hh, rows, :])).astype(o_ref.dtype)
        return carry

    lax.fori_loop(0, ts // rt, out_tile, 0)


def _delta(proj3, conv_w4, alog_rep, dtb_rep, norm_row, *, batch, seq, ts=2048, group=8, rt=256):
    t = proj3.shape[1]
    ns = seq // ts
    hps = HEADS_PER_STEP
    col = lambda off: pl.BlockSpec((hps, ts, LANES),
                                   lambda b, i, hp: (off // hps + hp, b * ns + i, 0))
    rep = pl.BlockSpec((N_GROUPS, LANES), lambda b, i, hp: (0, 0))
    n_slots = 2 * hps * group
    return pl.pallas_call(
        functools.partial(_delta_kernel, ts=ts, group=group, rt=rt),
        out_shape=jax.ShapeDtypeStruct((N_GROUPS, t, LANES), BF16),
        grid=(batch, ns, N_GROUPS // hps),
        in_specs=[
            col(CB_Q), col(CB_K), col(CB_V), col(CB_Z),
            pl.BlockSpec((1, ts, LANES), lambda b, i, hp: (CB_BA, b * ns + i, 0)),
            pl.BlockSpec((SHORT_CONV_WIDTH, 3, N_GROUPS, LANES), lambda b, i, hp: (0, 0, 0, 0)),
            rep, rep,
            pl.BlockSpec((1, LANES), lambda b, i, hp: (0, 0)),
        ],
        out_specs=pl.BlockSpec((hps, ts, LANES), lambda b, i, hp: (hp, b * ns + i, 0)),
        scratch_shapes=[
            pltpu.VMEM((3, hps, DN_HALO + ts, LANES), F32),
            pltpu.VMEM((N_GROUPS, 3, DN_HALO, LANES), F32),
            pltpu.VMEM((hps, ts, LANES), F32),
            pltpu.VMEM((hps, ts, LANES), F32),
            pltpu.VMEM((hps, ts, LANES), BF16),
            pltpu.VMEM((N_GROUPS, 1, ts), F32),
            pltpu.VMEM((N_GROUPS, 1, ts), F32),
            pltpu.VMEM((N_GROUPS, 1, ts), F32),
            pltpu.VMEM((N_GROUPS, 1, ts), F32),
            pltpu.VMEM((N_LEVELS, CHUNK, CHUNK), F32),
            pltpu.VMEM((n_slots, CHUNK, LANES), F32),
            pltpu.VMEM((n_slots, 2 * CHUNK, LANES), BF16),
            pltpu.VMEM((n_slots, LANES, CHUNK), BF16),
            pltpu.VMEM((n_slots, CHUNK, CHUNK), BF16),
            pltpu.VMEM((hps, ts, LANES), F32),
            pltpu.VMEM((N_GROUPS, LANES, LANES), F32),
        ],
        compiler_params=pltpu.CompilerParams(
            dimension_semantics=("arbitrary", "arbitrary", "arbitrary"),
            vmem_limit_bytes=VMEM_LIMIT),
        name="delta",
    )(proj3, proj3, proj3, proj3, proj3, conv_w4, alog_rep, dtb_rep, norm_row)


def _mlp_kernel(x_ref, val_ref, gate_ref, bv_ref, bg_ref, dw_ref, cb_ref, lg_ref, lb_ref,
                dn_ref, wout_ref, n2_ref, wup_ref, wdn_ref, fn_ref,
                o_ref, r_scr, h_scr, halo_scr, conv_scr, *, tm, rt, blocks_per_seq):
    i = pl.program_id(0)
    f = pl.program_id(1)
    n_blocks = pl.num_programs(0) - 1
    first_tap = CONF_HALO - (CONV_WIDTH - 1)

    def conf_part():
        seq_start = (i % blocks_per_seq) == 0
        h_scr[0:CONF_HALO, :] = jnp.where(seq_start, 0.0, halo_scr[f])
        a = val_ref[0] + bv_ref[0]
        g = gate_ref[0] + bg_ref[0]
        glu = a * _sigmoid(g)
        h_scr[CONF_HALO:CONF_HALO + tm, :] = glu
        zero_glu = _derived_zero(jnp.max(glu.reshape(tm // SUBLANES, SUBLANES, LANES), axis=0))
        halo_scr[f] = h_scr[tm:tm + CONF_HALO, :]
        slot = i % 2
        zeros = []
        for r in range(tm // rt):
            base = r * rt
            acc = dw_ref[0, 0:1, :] * h_scr[base + first_tap:base + first_tap + rt, :]
            for k in range(1, CONV_WIDTH):
                lo = base + first_tap + k
                acc = acc + dw_ref[0, k:k + 1, :] * h_scr[lo:lo + rt, :]
            y = acc + cb_ref[0]
            mu = jnp.mean(y, axis=-1, keepdims=True)
            cen = y - mu
            var = jnp.mean(cen * cen, axis=-1, keepdims=True)
            hn = cen * lax.rsqrt(var + EPS) * lg_ref[0] + lb_ref[0]
            out = _silu(hn)
            conv_scr[slot, f, base:base + rt, :] = out.astype(BF16)
            zeros.append(_derived_zero(out))
        return zero_glu, zeros

    def mlp_head():
        @pl.when(f == 0)
        def _():
            slot = (i + 1) % 2
            mix = jnp.concatenate([conv_scr[slot, c] for c in range(N_GROUPS)]
                                  + [dn_ref[c] for c in range(N_GROUPS)], axis=-1)
            h1 = x_ref[...] + jnp.dot(mix, wout_ref[...], preferred_element_type=F32)
            o_ref[...] = h1
            ms = jnp.mean(h1 * h1, axis=-1, keepdims=True)
            r_scr[...] = (h1 * lax.rsqrt(ms + EPS) * n2_ref[...]).astype(BF16)

    def mlp_body(zero_glu=None, zeros=None):
        m = jnp.dot(r_scr[...], wup_ref[...], preferred_element_type=F32)
        if zeros is not None:
            half = m.shape[1] // 2
            m = jnp.concatenate([
                jnp.concatenate([m[0:SUBLANES, 0:LANES] + zero_glu, m[SUBLANES:, 0:LANES]], axis=0),
                m[:, LANES:half],
                jnp.concatenate([m[0:rt, half:half + LANES] + zeros[0], m[rt:, half:half + LANES]],
                                axis=0),
                m[:, half + LANES:]], axis=1)
        act = jnp.square(jnp.maximum(m, 0.0)).astype(BF16)
        res = jnp.dot(act, wdn_ref[...], preferred_element_type=F32)
        if zeros is None:
            o_ref[...] += res
            return
        slab = o_ref.shape[1] // len(zeros)
        o_ref[:, 0:slab] += res[:, 0:slab]
        for n in range(1, len(zeros)):
            c0, r0 = n * slab, n * rt
            o_ref[:, c0 + LANES:c0 + slab] += res[:, c0 + LANES:c0 + slab]
            o_ref[0:r0, c0:c0 + LANES] += res[0:r0, c0:c0 + LANES]
            o_ref[r0:r0 + rt, c0:c0 + LANES] += res[r0:r0 + rt, c0:c0 + LANES] + zeros[n]
            if r0 + rt < tm:
                o_ref[r0 + rt:tm, c0:c0 + LANES] += res[r0 + rt:tm, c0:c0 + LANES]

    def mlp_tail():
        @pl.when(f == pl.num_programs(1) - 1)
        def _():
            h2 = o_ref[...]
            ms = jnp.mean(h2 * h2, axis=-1, keepdims=True)
            o_ref[...] = h2 * lax.rsqrt(ms + EPS) * fn_ref[...]

    @pl.when(i == 0)
    def _():
        @pl.when(f == 0)
        def _():
            halo_scr[...] = jnp.zeros_like(halo_scr)
        conf_part()

    @pl.when(jnp.logical_and(i > 0, i < n_blocks))
    def _():
        mlp_head()
        mlp_body(*conf_part())
        mlp_tail()

    @pl.when(i == n_blocks)
    def _():
        mlp_head()
        mlp_body()
        mlp_tail()


def _mlp(x2, proj3, b_glu, dw_w, dw_b, ln_g, ln_b, dn_o, w_out_b, n2, w_up_b, w_dn_b, fn,
         *, seq, tm=512, rt=64):
    t, d = x2.shape
    dff = w_up_b.shape[1]
    nb = t // tm
    nf = N_GROUPS
    tf = dff // nf
    prev = lambda i: jnp.maximum(i - 1, 0)
    nxt = lambda i: jnp.minimum(i, nb - 1)
    vec = lambda off: pl.BlockSpec((1, 1, LANES), lambda i, f: (off + f, 0, 0))
    return pl.pallas_call(
        functools.partial(_mlp_kernel, tm=tm, rt=rt, blocks_per_seq=seq // tm),
        out_shape=jax.ShapeDtypeStruct((t, d), F32),
        grid=(nb + 1, nf),
        in_specs=[
            pl.BlockSpec((tm, d), lambda i, f: (prev(i), 0)),
            pl.BlockSpec((1, tm, LANES), lambda i, f: (CB_VAL + f, nxt(i), 0)),
            pl.BlockSpec((1, tm, LANES), lambda i, f: (CB_GATE + f, nxt(i), 0)),
            vec(0), vec(N_GROUPS),
            pl.BlockSpec((1, CONV_WIDTH, LANES), lambda i, f: (f, 0, 0)),
            vec(0), vec(0), vec(0),
            pl.BlockSpec((N_GROUPS, tm, LANES), lambda i, f: (0, prev(i), 0)),
            pl.BlockSpec((d, d), lambda i, f: (0, 0), pipeline_mode=pl.Buffered(1)),
            pl.BlockSpec((1, d), lambda i, f: (0, 0)),
            pl.BlockSpec((d, tf), lambda i, f: (0, f)),
            pl.BlockSpec((tf, d), lambda i, f: (f, 0)),
            pl.BlockSpec((1, d), lambda i, f: (0, 0)),
        ],
        out_specs=pl.BlockSpec((tm, d), lambda i, f: (prev(i), 0)),
        scratch_shapes=[
            pltpu.VMEM((tm, d), BF16),
            pltpu.VMEM((CONF_HALO + tm, LANES), F32),
            pltpu.VMEM((N_GROUPS, CONF_HALO, LANES), F32),
            pltpu.VMEM((2, N_GROUPS, tm, LANES), BF16),
        ],
        compiler_params=pltpu.CompilerParams(
            dimension_semantics=("arbitrary", "arbitrary"),
            vmem_limit_bytes=VMEM_LIMIT),
        name="mlp",
    )(x2, proj3, proj3, b_glu, b_glu, dw_w, dw_b, ln_g, ln_b, dn_o,
      w_out_b, n2, w_up_b, w_dn_b, fn)


def _lane_rep(vals):
    return jnp.broadcast_to(vals.astype(F32)[:, None], (vals.shape[0], LANES))


def kernel(x, norm1_w, w_in, b_glu, conf_dw_w, conf_dw_b, conf_ln_g, conf_ln_b, dn_conv_w,
           dn_a_log, dn_dt_bias, dn_norm_w, w_out, norm2_w, w_mlp_up, w_mlp_down, final_norm_w):
    batch, seq, d = x.shape
    assert norm1_w.shape[0] == 1, "single-layer block"
    h = x.reshape(batch * seq, d)

    d_in = w_in.shape[2]
    w_in_p = jnp.pad(w_in[0], ((0, 0), (0, N_COLBLOCKS * LANES - d_in))).astype(BF16)
    proj3 = _inproj(h, norm1_w[0].reshape(1, d), w_in_p)

    conv_w4 = dn_conv_w[0].reshape(SHORT_CONV_WIDTH, 3, N_GROUPS, LANES)
    dn_o = _delta(proj3, conv_w4, _lane_rep(dn_a_log[0]), _lane_rep(dn_dt_bias[0]),
                  dn_norm_w[0].reshape(1, LANES), batch=batch, seq=seq)

    grp = lambda v: v.reshape(-1, 1, LANES)
    dw_w = conf_dw_w[0].reshape(CONV_WIDTH, N_GROUPS, LANES).transpose(1, 0, 2)
    out = _mlp(h, proj3, grp(b_glu[0]), dw_w, grp(conf_dw_b[0]), grp(conf_ln_g[0]),
               grp(conf_ln_b[0]), dn_o, w_out[0].astype(BF16), norm2_w[0].reshape(1, d),
               w_mlp_up[0].astype(BF16), w_mlp_down[0].astype(BF16),
               final_norm_w.reshape(1, d), seq=seq)
    return out.reshape(batch, seq, d)
```

```python
import functools

import jax
import jax.numpy as jnp
from jax import lax
from jax.experimental import pallas as pl
from jax.experimental.pallas import tpu as pltpu

F32 = jnp.float32
BF16 = jnp.bfloat16

EPS = 1e-6
LANES = 128
SUBLANES = 8
CONV_WIDTH = 31
SHORT_CONV_WIDTH = 4
N_GROUPS = 8
CONF_HALO = 32
DN_HALO = SUBLANES
CHUNK = 128
N_LEVELS = CHUNK.bit_length() - 1
HEADS_PER_STEP = 2

CB_VAL, CB_GATE, CB_Q, CB_K, CB_V, CB_Z, CB_BA = 0, 8, 16, 24, 32, 40, 48
N_COLBLOCKS = 50

VMEM_LIMIT = 56 * 1024 * 1024


def _sigmoid(x):
    return 0.5 * jnp.tanh(0.5 * x) + 0.5


def _silu(x):
    return x * _sigmoid(x)


def _derived_zero(x):
    bits = lax.bitcast_convert_type(x, jnp.uint32)
    bits = lax.shift_right_logical(lax.shift_right_logical(bits, jnp.uint32(16)), jnp.uint32(16))
    return lax.bitcast_convert_type(bits, F32)


def _tile_zero(x):
    return _derived_zero(jnp.max(x.reshape(-1, SUBLANES, x.shape[-1]), axis=0))


def _add_tile(x, tile):
    n = tile.shape[0]
    return jnp.concatenate([x[0:n] + tile, x[n:]], axis=0)


def _softplus(x):
    return jnp.maximum(x, 0.0) + jnp.log1p(jnp.exp(-jnp.abs(x)))


def _inproj_kernel(x_ref, nw_ref, w_ref, o_ref, u_scr):
    @pl.when(pl.program_id(1) == 0)
    def _():
        x = x_ref[...]
        ms = jnp.mean(x * x, axis=-1, keepdims=True)
        u_scr[...] = (x * lax.rsqrt(ms + EPS) * nw_ref[...]).astype(BF16)

    res = jnp.dot(u_scr[...], w_ref[...], preferred_element_type=F32)
    for c in range(o_ref.shape[0]):
        o_ref[c] = res[:, c * LANES:(c + 1) * LANES]


def _inproj(x2, norm_w, w_in_p, *, tm=1024, tn=1280):
    t, d = x2.shape
    n = w_in_p.shape[1]
    return pl.pallas_call(
        _inproj_kernel,
        out_shape=jax.ShapeDtypeStruct((n // LANES, t, LANES), F32),
        grid=(t // tm, n // tn),
        in_specs=[
            pl.BlockSpec((tm, d), lambda i, j: (i, 0)),
            pl.BlockSpec((1, d), lambda i, j: (0, 0)),
            pl.BlockSpec((d, tn), lambda i, j: (0, j)),
        ],
        out_specs=pl.BlockSpec((tn // LANES, tm, LANES), lambda i, j: (j, i, 0)),
        scratch_shapes=[pltpu.VMEM((tm, d), BF16)],
        compiler_params=pltpu.CompilerParams(
            dimension_semantics=("arbitrary", "arbitrary"),
            vmem_limit_bytes=VMEM_LIMIT),
        name="inproj",
    )(x2, norm_w, w_in_p)


def _delta_kernel(q_ref, k_ref, v_ref, z_ref, ba_ref, cw_ref, alog_ref, dtb_ref, nw_ref,
                  o_ref,
                  x_scr, halo_scr, q_scr, k_scr, v_scr,
                  beta_scr, cs_scr, f_scr, ld_scr, mask_scr,
                  u_scr, wq_scr, kgt_scr, intra_scr, o_scr, s_scr, *, ts, group, rt):
    i = pl.program_id(1)
    hp = pl.program_id(2)
    heads = [HEADS_PER_STEP * hp + hh for hh in range(HEADS_PER_STEP)]
    nc = ts // CHUNK

    r_io = lax.broadcasted_iota(jnp.int32, (CHUNK, CHUNK), 0)
    c_io = lax.broadcasted_iota(jnp.int32, (CHUNK, CHUNK), 1)
    causal = r_io >= c_io
    strict = r_io > c_io
    eye = jnp.where(r_io == c_io, 1.0, 0.0).astype(F32)

    @pl.when(hp == 0)
    def _():
        upper = jnp.where(r_io <= c_io, 1.0, 0.0).astype(F32)
        ones = jnp.ones((CHUNK, CHUNK), F32)
        for lvl in range(N_LEVELS):
            same = (r_io >> (lvl + 1)) == (c_io >> (lvl + 1))
            lower_left = jnp.where(((r_io >> lvl) & 1) == 1, 1.0, 0.0) * \
                jnp.where(((c_io >> lvl) & 1) == 0, 1.0, 0.0)
            mask_scr[lvl] = jnp.where(same, lower_left, 0.0).astype(F32)
        ba_t = ba_ref[0].T
        beta = _sigmoid(ba_t[0:N_GROUPS, :])
        a_raw = ba_t[N_GROUPS:2 * N_GROUPS, :]
        neg_a = -jnp.exp(alog_ref[...])
        for c in range(nc):
            cols = slice(c * CHUNK, (c + 1) * CHUNK)
            g_c = neg_a * _softplus(a_raw[:, cols] + dtb_ref[...])
            cs = jnp.dot(g_c, upper, precision=lax.Precision.HIGHEST,
                         preferred_element_type=F32)
            tot = jnp.dot(g_c, ones, precision=lax.Precision.HIGHEST,
                          preferred_element_type=F32)
            f_c = jnp.exp(tot - cs) * beta[:, cols]
            ld_c = jnp.exp(tot)
            for hh in range(N_GROUPS):
                one = slice(hh, hh + 1)
                beta_scr[hh, :, cols] = beta[one, cols]
                cs_scr[hh, :, cols] = cs[one, :]
                f_scr[hh, :, cols] = f_c[one, :]
                ld_scr[hh, :, cols] = ld_c[one, :]

    @pl.when(i == 0)
    def _():
        for h in heads:
            halo_scr[h] = jnp.zeros((3, DN_HALO, LANES), F32)
            s_scr[h] = jnp.zeros((LANES, LANES), F32)

    refs = (q_ref, k_ref, v_ref)
    back = SHORT_CONV_WIDTH - 1
    for hh, h in enumerate(heads):
        for j, ref in enumerate(refs):
            x_scr[j, hh, 0:DN_HALO, :] = halo_scr[h, j]
            x_scr[j, hh, DN_HALO:DN_HALO + rt, :] = ref[hh, 0:rt, :]
            halo_scr[h, j] = ref[hh, ts - DN_HALO:ts, :]

    def conv_rows(rows, load):
        tops = []
        for hh, h in enumerate(heads):
            ys = []
            for j in range(3):
                hv = (0.5 * cw_ref[0, j, pl.ds(h, 1), :]) * load(j, hh, 0)
                for m in range(1, SHORT_CONV_WIDTH):
                    hv = hv + (0.5 * cw_ref[m, j, pl.ds(h, 1), :]) * load(j, hh, m)
                ys.append(hv + hv * jnp.tanh(hv))
            qn = ys[0] * (lax.rsqrt(jnp.sum(ys[0] * ys[0], axis=-1, keepdims=True) + EPS)
                          * (LANES ** -0.5))
            kn = ys[1] * lax.rsqrt(jnp.sum(ys[1] * ys[1], axis=-1, keepdims=True) + EPS)
            q_scr[hh, rows, :] = qn
            k_scr[hh, rows, :] = kn
            v_scr[hh, rows, :] = ys[2].astype(BF16)
            tops.append(jnp.maximum(jnp.maximum(qn, kn), ys[2]))
        return _tile_zero(functools.reduce(jnp.maximum, tops))

    conv_rows(pl.ds(0, rt),
              lambda j, hh, m: x_scr[j, hh, DN_HALO - back + m:DN_HALO - back + m + rt, :])

    def conv_tile(r, carry):
        base = pl.multiple_of(r * rt, rt)
        conv_rows(pl.ds(base, rt),
                  lambda j, hh, m: refs[j][hh, pl.ds(base - back + m, rt), :])
        return carry

    n_tiles = ts // rt
    lax.fori_loop(1, n_tiles // 2, conv_tile, 0)

    def conv_late(ties):
        for r in range(n_tiles // 2, n_tiles):
            base = r * rt
            ties.append(conv_rows(
                pl.ds(base, rt),
                lambda j, hh, m: refs[j][hh, base - back + m:base - back + m + rt, :]))
            yield
            yield
            yield

    def prep_group(first_chunk, slot0, ties=()):
        def take_ties():
            while ties:
                it = items[len(ties) % len(items)]
                it["x"] = _add_tile(it["x"], ties.pop())

        items = []
        for hh, h in enumerate(heads):
            for g in range(group):
                slot = slot0 + hh * group + g
                r0 = pl.multiple_of((first_chunk + g) * CHUNK, CHUNK)
                rows = pl.ds(r0, CHUNK)
                k_c = k_scr[hh, rows, :]
                qs_c = q_scr[hh, rows, :]
                cs_r = jnp.broadcast_to(cs_scr[h, :, rows], (CHUNK, CHUNK))
                beta_r = jnp.broadcast_to(beta_scr[h, :, rows], (CHUNK, CHUNK))
                cs_c = cs_r.T
                db = jnp.exp(jnp.where(causal, cs_c - cs_r, -jnp.inf)) * beta_r
                kq = lax.dot_general(
                    jnp.concatenate([k_c, qs_c], axis=0).astype(BF16), k_c.astype(BF16),
                    (((1,), (1,)), ((), ())), preferred_element_type=F32)
                a_mat = jnp.where(strict, kq[:CHUNK] * db, 0.0)
                intra_scr[slot] = (kq[CHUNK:] * db).astype(BF16)
                eg = jnp.exp(cs_c)
                wq_scr[slot, CHUNK:2 * CHUNK, :] = (qs_c * eg).astype(BF16)
                kgt_scr[slot] = (k_c.T * f_scr[h, :, rows]).astype(BF16)
                rhs = jnp.concatenate([v_scr[hh, rows, :], (k_c * eg).astype(BF16)], axis=1)
                items.append(dict(a=a_mat, x=eye - a_mat * mask_scr[0], rhs=rhs, slot=slot))
        yield
        for lvl in range(1, N_LEVELS):
            take_ties()
            for it in items:
                it["xb"] = it["x"].astype(BF16)
                it["mx"] = jnp.dot((it["a"] * mask_scr[lvl]).astype(BF16), it["xb"],
                                   preferred_element_type=F32)
            yield
            for it in items:
                it["x"] = it["x"] - jnp.dot(it["xb"], it["mx"].astype(BF16),
                                            preferred_element_type=F32)
            yield
        take_ties()
        for it in items:
            uw = jnp.dot(it["x"].astype(BF16), it["rhs"], preferred_element_type=F32)
            u_scr[it["slot"]] = uw[:, :LANES]
            wq_scr[it["slot"], 0:CHUNK, :] = uw[:, LANES:].astype(BF16)
        yield

    def step_group(first_chunk, slot0, states, ties=()):
        for g in range(group):
            r0 = pl.multiple_of((first_chunk + g) * CHUNK, CHUNK)
            rows = pl.ds(r0, CHUNK)
            ws = [jnp.dot(wq_scr[slot0 + hh * group + g], states[hh].astype(BF16),
                          preferred_element_type=F32) for hh in range(HEADS_PER_STEP)]
            yield
            for hh, h in enumerate(heads):
                slot = slot0 + hh * group + g
                v_new = (u_scr[slot] - ws[hh][:CHUNK]).astype(BF16)
                o_c = ws[hh][CHUNK:] + jnp.dot(intra_scr[slot], v_new,
                                               preferred_element_type=F32)
                if ties:
                    o_c = _add_tile(o_c, ties.pop())
                o_scr[hh, rows, :] = o_c
                states[hh] = states[hh] * ld_scr[h, :, rows] + jnp.dot(
                    kgt_scr[slot], v_new, preferred_element_type=F32)
            yield

    def interleave(*gens):
        live = list(gens)
        while live:
            for gen in list(live):
                try:
                    next(gen)
                except StopIteration:
                    live.remove(gen)

    assert nc == 2 * group and n_tiles % 2 == 0
    set_size = HEADS_PER_STEP * group
    ties = []
    interleave(conv_late(ties), prep_group(0, 0, ties))
    states = [s_scr[h] for h in heads]
    interleave(step_group(0, 0, states), prep_group(group, set_size))

    def out_rows(rows):
        tops = []
        for hh in range(HEADS_PER_STEP):
            o = o_scr[hh, rows, :]
            o = o * lax.rsqrt(jnp.mean(o * o, axis=-1, keepdims=True) + EPS) * nw_ref[...]
            res = o * _silu(z_ref[hh, rows, :])
            o_ref[hh, rows, :] = res.astype(o_ref.dtype)
            tops.append(res)
        return _tile_zero(functools.reduce(jnp.maximum, tops))

    def out_early(ties):
        for r in range(n_tiles // 2):
            ties.append(out_rows(pl.ds(r * rt, rt)))
            for _ in range(2 * group // (n_tiles // 2)):
                yield

    ties = []
    interleave(out_early(ties), step_group(group, set_size, states, ties))
    for hh, h in enumerate(heads):
        s_scr[h] = states[hh]

    def out_tile(r, carry):
        out_rows(pl.ds(pl.multiple_of(r * rt, rt), rt))
        return carry

    lax.fori_loop(n_tiles // 2, n_tiles, out_tile, 0)


def _delta(proj3, conv_w4, alog_rep, dtb_rep, norm_row, *, batch, seq, ts=2048, group=8, rt=256):
    t = proj3.shape[1]
    ns = seq // ts
    hps = HEADS_PER_STEP
    col = lambda off: pl.BlockSpec((hps, ts, LANES),
                                   lambda b, i, hp: (off // hps + hp, b * ns + i, 0))
    rep = pl.BlockSpec((N_GROUPS, LANES), lambda b, i, hp: (0, 0))
    n_slots = 2 * hps * group
    return pl.pallas_call(
        functools.partial(_delta_kernel, ts=ts, group=group, rt=rt),
        out_shape=jax.ShapeDtypeStruct((N_GROUPS, t, LANES), BF16),
        grid=(batch, ns, N_GROUPS // hps),
        in_specs=[
            col(CB_Q), col(CB_K), col(CB_V), col(CB_Z),
            pl.BlockSpec((1, ts, LANES), lambda b, i, hp: (CB_BA, b * ns + i, 0)),
            pl.BlockSpec((SHORT_CONV_WIDTH, 3, N_GROUPS, LANES), lambda b, i, hp: (0, 0, 0, 0)),
            rep, rep,
            pl.BlockSpec((1, LANES), lambda b, i, hp: (0, 0)),
        ],
        out_specs=pl.BlockSpec((hps, ts, LANES), lambda b, i, hp: (hp, b * ns + i, 0)),
        scratch_shapes=[
            pltpu.VMEM((3, hps, DN_HALO + rt, LANES), F32),
            pltpu.VMEM((N_GROUPS, 3, DN_HALO, LANES), F32),
            pltpu.VMEM((hps, ts, LANES), F32),
            pltpu.VMEM((hps, ts, LANES), F32),
            pltpu.VMEM((hps, ts, LANES), BF16),
            pltpu.VMEM((N_GROUPS, 1, ts), F32),
            pltpu.VMEM((N_GROUPS, 1, ts), F32),
            pltpu.VMEM((N_GROUPS, 1, ts), F32),
            pltpu.VMEM((N_GROUPS, 1, ts), F32),
            pltpu.VMEM((N_LEVELS, CHUNK, CHUNK), F32),
            pltpu.VMEM((n_slots, CHUNK, LANES), F32),
            pltpu.VMEM((n_slots, 2 * CHUNK, LANES), BF16),
            pltpu.VMEM((n_slots, LANES, CHUNK), BF16),
            pltpu.VMEM((n_slots, CHUNK, CHUNK), BF16),
            pltpu.VMEM((hps, ts, LANES), F32),
            pltpu.VMEM((N_GROUPS, LANES, LANES), F32),
        ],
        compiler_params=pltpu.CompilerParams(
            dimension_semantics=("arbitrary", "arbitrary", "arbitrary"),
            vmem_limit_bytes=VMEM_LIMIT),
        name="delta",
    )(proj3, proj3, proj3, proj3, proj3, conv_w4, alog_rep, dtb_rep, norm_row)


def _mlp_kernel(x_ref, val_ref, gate_ref, bv_ref, bg_ref, dw_ref, cb_ref, lg_ref, lb_ref,
                dn_ref, wout_ref, n2_ref, wup_ref, wdn_ref, fn_ref,
                o_ref, r_scr, h_scr, halo_scr, conv_scr, *, tm, rt, blocks_per_seq):
    i = pl.program_id(0)
    f = pl.program_id(1)
    n_blocks = pl.num_programs(0) - 1
    first_tap = CONF_HALO - (CONV_WIDTH - 1)

    def conf_part():
        seq_start = (i % blocks_per_seq) == 0
        h_scr[0:CONF_HALO, :] = jnp.where(seq_start, 0.0, halo_scr[f])
        a = val_ref[0] + bv_ref[0]
        g = gate_ref[0] + bg_ref[0]
        glu = a * _sigmoid(g)
        h_scr[CONF_HALO:CONF_HALO + tm, :] = glu
        zero_glu = _derived_zero(jnp.max(glu.reshape(tm // SUBLANES, SUBLANES, LANES), axis=0))
        halo_scr[f] = h_scr[tm:tm + CONF_HALO, :]
        slot = i % 2
        zeros = []
        for r in range(tm // rt):
            base = r * rt
            acc = dw_ref[0, 0:1, :] * h_scr[base + first_tap:base + first_tap + rt, :]
            for k in range(1, CONV_WIDTH):
                lo = base + first_tap + k
                acc = acc + dw_ref[0, k:k + 1, :] * h_scr[lo:lo + rt, :]
            y = acc + cb_ref[0]
            mu = jnp.mean(y, axis=-1, keepdims=True)
            cen = y - mu
            var = jnp.mean(cen * cen, axis=-1, keepdims=True)
            hn = cen * lax.rsqrt(var + EPS) * lg_ref[0] + lb_ref[0]
            out = _silu(hn)
            conv_scr[slot, f, base:base + rt, :] = out.astype(BF16)
            zeros.append(_derived_zero(out))
        return zero_glu, zeros

    def mlp_head():
        @pl.when(f == 0)
        def _():
            slot = (i + 1) % 2
            mix = jnp.concatenate([conv_scr[slot, c] for c in range(N_GROUPS)]
                                  + [dn_ref[c] for c in range(N_GROUPS)], axis=-1)
            h1 = x_ref[...] + jnp.dot(mix, wout_ref[...], preferred_element_type=F32)
            o_ref[...] = h1
            ms = jnp.mean(h1 * h1, axis=-1, keepdims=True)
            r_scr[...] = (h1 * lax.rsqrt(ms + EPS) * n2_ref[...]).astype(BF16)

    def mlp_body(zero_glu=None, zeros=None):
        m = jnp.dot(r_scr[...], wup_ref[...], preferred_element_type=F32)
        if zeros is not None:
            half = m.shape[1] // 2
            m = jnp.concatenate([
                jnp.concatenate([m[0:SUBLANES, 0:LANES] + zero_glu, m[SUBLANES:, 0:LANES]], axis=0),
                m[:, LANES:half],
                jnp.concatenate([m[0:rt, half:half + LANES] + zeros[0], m[rt:, half:half + LANES]],
                                axis=0),
                m[:, half + LANES:]], axis=1)
        act = jnp.square(jnp.maximum(m, 0.0)).astype(BF16)
        res = jnp.dot(act, wdn_ref[...], preferred_element_type=F32)
        if zeros is None:
            o_ref[...] += res
            return
        slab = o_ref.shape[1] // len(zeros)
        o_ref[:, 0:slab] += res[:, 0:slab]
        for n in range(1, len(zeros)):
            c0, r0 = n * slab, n * rt
            o_ref[:, c0 + LANES:c0 + slab] += res[:, c0 + LANES:c0 + slab]
            o_ref[0:r0, c0:c0 + LANES] += res[0:r0, c0:c0 + LANES]
            o_ref[r0:r0 + rt, c0:c0 + LANES] += res[r0:r0 + rt, c0:c0 + LANES] + zeros[n]
            if r0 + rt < tm:
                o_ref[r0 + rt:tm, c0:c0 + LANES] += res[r0 + rt:tm, c0:c0 + LANES]

    def mlp_tail():
        @pl.when(f == pl.num_programs(1) - 1)
        def _():
            h2 = o_ref[...]
            ms = jnp.mean(h2 * h2, axis=-1, keepdims=True)
            o_ref[...] = h2 * lax.rsqrt(ms + EPS) * fn_ref[...]

    @pl.when(i == 0)
    def _():
        @pl.when(f == 0)
        def _():
            halo_scr[...] = jnp.zeros_like(halo_scr)
        conf_part()

    @pl.when(jnp.logical_and(i > 0, i < n_blocks))
    def _():
        mlp_head()
        mlp_body(*conf_part())
        mlp_tail()

    @pl.when(i == n_blocks)
    def _():
        mlp_head()
        mlp_body()
        mlp_tail()


def _mlp(x2, proj3, b_glu, dw_w, dw_b, ln_g, ln_b, dn_o, w_out_b, n2, w_up_b, w_dn_b, fn,
         *, seq, tm=512, rt=64):
    t, d = x2.shape
    dff = w_up_b.shape[1]
    nb = t // tm
    nf = N_GROUPS
    tf = dff // nf
    prev = lambda i: jnp.maximum(i - 1, 0)
    nxt = lambda i: jnp.minimum(i, nb - 1)
    vec = lambda off: pl.BlockSpec((1, 1, LANES), lambda i, f: (off + f, 0, 0))
    return pl.pallas_call(
        functools.partial(_mlp_kernel, tm=tm, rt=rt, blocks_per_seq=seq // tm),
        out_shape=jax.ShapeDtypeStruct((t, d), F32),
        grid=(nb + 1, nf),
        in_specs=[
            pl.BlockSpec((tm, d), lambda i, f: (prev(i), 0)),
            pl.BlockSpec((1, tm, LANES), lambda i, f: (CB_VAL + f, nxt(i), 0)),
            pl.BlockSpec((1, tm, LANES), lambda i, f: (CB_GATE + f, nxt(i), 0)),
            vec(0), vec(N_GROUPS),
            pl.BlockSpec((1, CONV_WIDTH, LANES), lambda i, f: (f, 0, 0)),
            vec(0), vec(0), vec(0),
            pl.BlockSpec((N_GROUPS, tm, LANES), lambda i, f: (0, prev(i), 0)),
            pl.BlockSpec((d, d), lambda i, f: (0, 0), pipeline_mode=pl.Buffered(1)),
            pl.BlockSpec((1, d), lambda i, f: (0, 0)),
            pl.BlockSpec((d, tf), lambda i, f: (0, f)),
            pl.BlockSpec((tf, d), lambda i, f: (f, 0)),
            pl.BlockSpec((1, d), lambda i, f: (0, 0)),
        ],
        out_specs=pl.BlockSpec((tm, d), lambda i, f: (prev(i), 0)),
        scratch_shapes=[
            pltpu.VMEM((tm, d), BF16),
            pltpu.VMEM((CONF_HALO + tm, LANES), F32),
            pltpu.VMEM((N_GROUPS, CONF_HALO, LANES), F32),
            pltpu.VMEM((2, N_GROUPS, tm, LANES), BF16),
        ],
        compiler_params=pltpu.CompilerParams(
            dimension_semantics=("arbitrary", "arbitrary"),
            vmem_limit_bytes=VMEM_LIMIT),
        name="mlp",
    )(x2, proj3, proj3, b_glu, b_glu, dw_w, dw_b, ln_g, ln_b, dn_o,
      w_out_b, n2, w_up_b, w_dn_b, fn)


def _lane_rep(vals):
    return jnp.broadcast_to(vals.astype(F32)[:, None], (vals.shape[0], LANES))


def kernel(x, norm1_w, w_in, b_glu, conf_dw_w, conf_dw_b, conf_ln_g, conf_ln_b, dn_conv_w,
           dn_a_log, dn_dt_bias, dn_norm_w, w_out, norm2_w, w_mlp_up, w_mlp_down, final_norm_w):
    batch, seq, d = x.shape
    assert norm1_w.shape[0] == 1, "single-layer block"
    h = x.reshape(batch * seq, d)

    d_in = w_in.shape[2]
    w_in_p = jnp.pad(w_in[0].astype(BF16), ((0, 0), (0, N_COLBLOCKS * LANES - d_in)))
    proj3 = _inproj(h, norm1_w[0].reshape(1, d), w_in_p)

    conv_w4 = dn_conv_w[0].reshape(SHORT_CONV_WIDTH, 3, N_GROUPS, LANES)
    dn_o = _delta(proj3, conv_w4, _lane_rep(dn_a_log[0]), _lane_rep(dn_dt_bias[0]),
                  dn_norm_w[0].reshape(1, LANES), batch=batch, seq=seq)

    grp = lambda v: v.reshape(-1, 1, LANES)
    dw_w = conf_dw_w[0].reshape(CONV_WIDTH, N_GROUPS, LANES).transpose(1, 0, 2)
    out = _mlp(h, proj3, grp(b_glu[0]), dw_w, grp(conf_dw_b[0]), grp(conf_ln_g[0]),
               grp(conf_ln_b[0]), dn_o, w_out[0].astype(BF16), norm2_w[0].reshape(1, d),
               w_mlp_up[0].astype(BF16), w_mlp_down[0].astype(BF16),
               final_norm_w.reshape(1, d), seq=seq)
    return out.reshape(batch, seq, d)
```

```python
import functools

import jax
import jax.numpy as jnp
from jax import lax
from jax.experimental import pallas as pl
from jax.experimental.pallas import tpu as pltpu

F32 = jnp.float32
BF16 = jnp.bfloat16

EPS = 1e-6
LANES = 128
SUBLANES = 8
CONV_WIDTH = 31
SHORT_CONV_WIDTH = 4
N_GROUPS = 8
CONF_HALO = 32
DN_HALO = SUBLANES
CHUNK = 128
N_LEVELS = CHUNK.bit_length() - 1
HEADS_PER_STEP = 2

CB_VAL, CB_GATE, CB_Q, CB_K, CB_V, CB_Z, CB_BA = 0, 8, 16, 24, 32, 40, 48
N_COLBLOCKS = 50

VMEM_LIMIT = 56 * 1024 * 1024


def _sigmoid(x):
    return 0.5 * jnp.tanh(0.5 * x) + 0.5


def _silu(x):
    return x * _sigmoid(x)


def _derived_zero(x):
    bits = lax.bitcast_convert_type(x, jnp.uint32)
    bits = lax.shift_right_logical(lax.shift_right_logical(bits, jnp.uint32(16)), jnp.uint32(16))
    return lax.bitcast_convert_type(bits, F32)


def _tile_zero(x):
    return _derived_zero(jnp.max(x.reshape(-1, SUBLANES, x.shape[-1]), axis=0))


def _add_tile(x, tile):
    n = tile.shape[0]
    return jnp.concatenate([x[0:n] + tile, x[n:]], axis=0)


def _softplus(x):
    return jnp.maximum(x, 0.0) + jnp.log1p(jnp.exp(-jnp.abs(x)))


def _inproj_kernel(x_ref, nw_ref, w_ref, o_ref, u_scr):
    @pl.when(pl.program_id(1) == 0)
    def _():
        x = x_ref[...]
        ms = jnp.mean(x * x, axis=-1, keepdims=True)
        u_scr[...] = (x * lax.rsqrt(ms + EPS) * nw_ref[...]).astype(BF16)

    res = jnp.dot(u_scr[...], w_ref[...], preferred_element_type=F32)
    for c in range(o_ref.shape[0]):
        o_ref[c] = res[:, c * LANES:(c + 1) * LANES]


def _inproj(x2, norm_w, w_in_p, *, tm=1024, tn=1280):
    t, d = x2.shape
    n = w_in_p.shape[1]
    return pl.pallas_call(
        _inproj_kernel,
        out_shape=jax.ShapeDtypeStruct((n // LANES, t, LANES), F32),
        grid=(t // tm, n // tn),
        in_specs=[
            pl.BlockSpec((tm, d), lambda i, j: (i, 0)),
            pl.BlockSpec((1, d), lambda i, j: (0, 0)),
            pl.BlockSpec((d, tn), lambda i, j: (0, j)),
        ],
        out_specs=pl.BlockSpec((tn // LANES, tm, LANES), lambda i, j: (j, i, 0)),
        scratch_shapes=[pltpu.VMEM((tm, d), BF16)],
        compiler_params=pltpu.CompilerParams(
            dimension_semantics=("arbitrary", "arbitrary"),
            vmem_limit_bytes=VMEM_LIMIT),
        name="inproj",
    )(x2, norm_w, w_in_p)


def _delta_kernel(q_ref, k_ref, v_ref, z_ref, ba_ref, cw_ref, alog_ref, dtb_ref, nw_ref,
                  o_ref,
                  x_scr, halo_scr, q_scr, k_scr, v_scr,
                  beta_scr, cs_scr, f_scr, ld_scr, mask_scr,
                  u_scr, wq_scr, kgt_scr, intra_scr, ldc_scr, o_scr, s_scr,
                  *, ts, group, rt, n_items, blocks_per_seq):
    step = pl.program_id(0)
    item = jnp.minimum(step, n_items - 1)
    prev = jnp.maximum(step - 1, 0)
    pairs = N_GROUPS // HEADS_PER_STEP
    hp = item % pairs
    i = (item // pairs) % blocks_per_seq
    heads = [HEADS_PER_STEP * hp + hh for hh in range(HEADS_PER_STEP)]
    heads_prev = [HEADS_PER_STEP * (prev % pairs) + hh for hh in range(HEADS_PER_STEP)]
    nc = ts // CHUNK

    r_io = lax.broadcasted_iota(jnp.int32, (CHUNK, CHUNK), 0)
    c_io = lax.broadcasted_iota(jnp.int32, (CHUNK, CHUNK), 1)
    causal = r_io >= c_io
    strict = r_io > c_io
    eye = jnp.where(r_io == c_io, 1.0, 0.0).astype(F32)

    @pl.when(hp == 0)
    def _():
        upper = jnp.where(r_io <= c_io, 1.0, 0.0).astype(F32)
        ones = jnp.ones((CHUNK, CHUNK), F32)
        for lvl in range(N_LEVELS):
            same = (r_io >> (lvl + 1)) == (c_io >> (lvl + 1))
            lower_left = jnp.where(((r_io >> lvl) & 1) == 1, 1.0, 0.0) * \
                jnp.where(((c_io >> lvl) & 1) == 0, 1.0, 0.0)
            mask_scr[lvl] = jnp.where(same, lower_left, 0.0).astype(F32)
        ba_t = ba_ref[0].T
        beta = _sigmoid(ba_t[0:N_GROUPS, :])
        a_raw = ba_t[N_GROUPS:2 * N_GROUPS, :]
        neg_a = -jnp.exp(alog_ref[...])
        for c in range(nc):
            cols = slice(c * CHUNK, (c + 1) * CHUNK)
            g_c = neg_a * _softplus(a_raw[:, cols] + dtb_ref[...])
            cs = jnp.dot(g_c, upper, precision=lax.Precision.HIGHEST,
                         preferred_element_type=F32)
            tot = jnp.dot(g_c, ones, precision=lax.Precision.HIGHEST,
                          preferred_element_type=F32)
            f_c = jnp.exp(tot - cs) * beta[:, cols]
            ld_c = jnp.exp(tot)
            for hh in range(N_GROUPS):
                one = slice(hh, hh + 1)
                beta_scr[hh, :, cols] = beta[one, cols]
                cs_scr[hh, :, cols] = cs[one, :]
                f_scr[hh, :, cols] = f_c[one, :]
                ld_scr[hh, :, cols] = ld_c[one, :]

    @pl.when(i == 0)
    def _():
        for h in heads:
            halo_scr[h] = jnp.zeros((3, DN_HALO, LANES), F32)
            s_scr[h] = jnp.zeros((LANES, LANES), F32)

    set_size = HEADS_PER_STEP * group

    @pl.when(step == 0)
    def _():
        late = slice(set_size, 2 * set_size)
        u_scr[late] = jnp.zeros((set_size,) + u_scr.shape[1:], F32)
        wq_scr[late] = jnp.zeros((set_size,) + wq_scr.shape[1:], BF16)
        kgt_scr[late] = jnp.zeros((set_size,) + kgt_scr.shape[1:], BF16)
        intra_scr[late] = jnp.zeros((set_size,) + intra_scr.shape[1:], BF16)
        ldc_scr[late] = jnp.zeros((set_size,) + ldc_scr.shape[1:], F32)
        o_scr[...] = jnp.zeros_like(o_scr)

    refs = (q_ref, k_ref, v_ref)
    back = SHORT_CONV_WIDTH - 1
    for hh, h in enumerate(heads):
        for j, ref in enumerate(refs):
            x_scr[j, hh, 0:DN_HALO, :] = halo_scr[h, j]
            x_scr[j, hh, DN_HALO:DN_HALO + rt, :] = ref[hh, 0:rt, :]
            halo_scr[h, j] = ref[hh, ts - DN_HALO:ts, :]

    def conv_rows(rows, load):
        tops = []
        for hh, h in enumerate(heads):
            ys = []
            for j in range(3):
                hv = (0.5 * cw_ref[0, j, pl.ds(h, 1), :]) * load(j, hh, 0)
                for m in range(1, SHORT_CONV_WIDTH):
                    hv = hv + (0.5 * cw_ref[m, j, pl.ds(h, 1), :]) * load(j, hh, m)
                ys.append(hv + hv * jnp.tanh(hv))
            qn = ys[0] * (lax.rsqrt(jnp.sum(ys[0] * ys[0], axis=-1, keepdims=True) + EPS)
                          * (LANES ** -0.5))
            kn = ys[1] * lax.rsqrt(jnp.sum(ys[1] * ys[1], axis=-1, keepdims=True) + EPS)
            q_scr[hh, rows, :] = qn
            k_scr[hh, rows, :] = kn
            v_scr[hh, rows, :] = ys[2].astype(BF16)
            tops.append(jnp.maximum(jnp.maximum(qn, kn), ys[2]))
        return _tile_zero(functools.reduce(jnp.maximum, tops))

    conv_rows(pl.ds(0, rt),
              lambda j, hh, m: x_scr[j, hh, DN_HALO - back + m:DN_HALO - back + m + rt, :])

    def conv_tile(r, carry):
        base = pl.multiple_of(r * rt, rt)
        conv_rows(pl.ds(base, rt),
                  lambda j, hh, m: refs[j][hh, pl.ds(base - back + m, rt), :])
        return carry

    n_tiles = ts // rt
    lax.fori_loop(1, n_tiles // 2, conv_tile, 0)

    def conv_late(ties):
        for r in range(n_tiles // 2, n_tiles):
            base = r * rt
            ties.append(conv_rows(
                pl.ds(base, rt),
                lambda j, hh, m: refs[j][hh, base - back + m:base - back + m + rt, :]))
            yield
            yield
            yield

    def prep_group(first_chunk, slot0, ties=()):
        def take_ties():
            while ties:
                it = items[len(ties) % len(items)]
                it["x"] = _add_tile(it["x"], ties.pop())

        items = []
        for hh, h in enumerate(heads):
            for g in range(group):
                slot = slot0 + hh * group + g
                r0 = pl.multiple_of((first_chunk + g) * CHUNK, CHUNK)
                rows = pl.ds(r0, CHUNK)
                k_c = k_scr[hh, rows, :]
                qs_c = q_scr[hh, rows, :]
                cs_r = jnp.broadcast_to(cs_scr[h, :, rows], (CHUNK, CHUNK))
                beta_r = jnp.broadcast_to(beta_scr[h, :, rows], (CHUNK, CHUNK))
                cs_c = cs_r.T
                db = jnp.exp(jnp.where(causal, cs_c - cs_r, -jnp.inf)) * beta_r
                kq = lax.dot_general(
                    jnp.concatenate([k_c, qs_c], axis=0).astype(BF16), k_c.astype(BF16),
                    (((1,), (1,)), ((), ())), preferred_element_type=F32)
                a_mat = jnp.where(strict, kq[:CHUNK] * db, 0.0)
                intra_scr[slot] = (kq[CHUNK:] * db).astype(BF16)
                eg = jnp.exp(cs_c)
                wq_scr[slot, CHUNK:2 * CHUNK, :] = (qs_c * eg).astype(BF16)
                kgt_scr[slot] = (k_c.T * f_scr[h, :, rows]).astype(BF16)
                ldc_scr[slot] = ld_scr[h, :, pl.ds(r0, LANES)]
                rhs = jnp.concatenate([v_scr[hh, rows, :], (k_c * eg).astype(BF16)], axis=1)
                items.append(dict(a=a_mat, x=eye - a_mat * mask_scr[0], rhs=rhs, slot=slot))
        yield
        for lvl in range(1, N_LEVELS):
            take_ties()
            for it in items:
                it["xb"] = it["x"].astype(BF16)
                it["mx"] = jnp.dot((it["a"] * mask_scr[lvl]).astype(BF16), it["xb"],
                                   preferred_element_type=F32)
            yield
            for it in items:
                it["x"] = it["x"] - jnp.dot(it["xb"], it["mx"].astype(BF16),
                                            preferred_element_type=F32)
            yield
        take_ties()
        for it in items:
            uw = jnp.dot(it["x"].astype(BF16), it["rhs"], preferred_element_type=F32)
            u_scr[it["slot"]] = uw[:, :LANES]
            wq_scr[it["slot"], 0:CHUNK, :] = uw[:, LANES:].astype(BF16)
        yield

    def step_group(first_chunk, slot0, states, ties=()):
        for g in range(group):
            r0 = pl.multiple_of((first_chunk + g) * CHUNK, CHUNK)
            rows = pl.ds(r0, CHUNK)
            ws = [jnp.dot(wq_scr[slot0 + hh * group + g], states[hh].astype(BF16),
                          preferred_element_type=F32) for hh in range(HEADS_PER_STEP)]
            yield
            for hh in range(HEADS_PER_STEP):
                slot = slot0 + hh * group + g
                v_new = (u_scr[slot] - ws[hh][:CHUNK]).astype(BF16)
                o_c = ws[hh][CHUNK:] + jnp.dot(intra_scr[slot], v_new,
                                               preferred_element_type=F32)
                if ties:
                    o_c = _add_tile(o_c, ties.pop())
                o_scr[hh, rows, :] = o_c
                states[hh] = states[hh] * ldc_scr[slot] + jnp.dot(
                    kgt_scr[slot], v_new, preferred_element_type=F32)
            yield

    def interleave(*gens):
        live = list(gens)
        while live:
            for gen in list(live):
                try:
                    next(gen)
                except StopIteration:
                    live.remove(gen)

    assert nc == 2 * group and n_tiles % 2 == 0

    def out_rows(rows):
        tops = []
        for hh in range(HEADS_PER_STEP):
            o = o_scr[hh, rows, :]
            o = o * lax.rsqrt(jnp.mean(o * o, axis=-1, keepdims=True) + EPS) * nw_ref[...]
            res = o * _silu(z_ref[hh, rows, :])
            o_ref[hh, rows, :] = res.astype(o_ref.dtype)
            tops.append(res)
        return _tile_zero(functools.reduce(jnp.maximum, tops))

    def out_tiles(ties):
        for r in range(n_tiles):
            ties.append(out_rows(pl.ds(r * rt, rt)))
            for _ in range(2 * group // n_tiles):
                yield

    conv_ties = []
    states = [s_scr[h] for h in heads_prev]
    interleave(conv_late(conv_ties), prep_group(0, 0, conv_ties),
               step_group(group, set_size, states))
    for hh, h in enumerate(heads_prev):
        s_scr[h] = states[hh]

    out_ties = []
    states = [s_scr[h] for h in heads]
    interleave(out_tiles(out_ties), step_group(0, 0, states, out_ties),
               prep_group(group, set_size))
    for hh, h in enumerate(heads):
        s_scr[h] = states[hh]


def _delta(proj3, conv_w4, alog_rep, dtb_rep, norm_row, *, batch, seq, ts=2048, group=8, rt=256):
    t = proj3.shape[1]
    ns = seq // ts
    hps = HEADS_PER_STEP
    pairs = N_GROUPS // hps
    n_items = batch * ns * pairs

    def cur(off):
        def index_map(s):
            it = jnp.minimum(s, n_items - 1)
            return (off // hps + it % pairs, it // pairs, 0)
        return index_map

    def prv(off):
        def index_map(s):
            it = jnp.maximum(s - 1, 0)
            return (off // hps + it % pairs, it // pairs, 0)
        return index_map

    blk = lambda index_map: pl.BlockSpec((hps, ts, LANES), index_map)
    rep = pl.BlockSpec((N_GROUPS, LANES), lambda s: (0, 0))
    n_slots = 2 * hps * group
    return pl.pallas_call(
        functools.partial(_delta_kernel, ts=ts, group=group, rt=rt, n_items=n_items,
                          blocks_per_seq=ns),
        out_shape=jax.ShapeDtypeStruct((N_GROUPS, t, LANES), BF16),
        grid=(n_items + 1,),
        in_specs=[
            blk(cur(CB_Q)), blk(cur(CB_K)), blk(cur(CB_V)), blk(prv(CB_Z)),
            pl.BlockSpec((1, ts, LANES),
                         lambda s: (CB_BA, jnp.minimum(s, n_items - 1) // pairs, 0)),
            pl.BlockSpec((SHORT_CONV_WIDTH, 3, N_GROUPS, LANES), lambda s: (0, 0, 0, 0)),
            rep, rep,
            pl.BlockSpec((1, LANES), lambda s: (0, 0)),
        ],
        out_specs=blk(prv(0)),
        scratch_shapes=[
            pltpu.VMEM((3, hps, DN_HALO + rt, LANES), F32),
            pltpu.VMEM((N_GROUPS, 3, DN_HALO, LANES), F32),
            pltpu.VMEM((hps, ts, LANES), F32),
            pltpu.VMEM((hps, ts, LANES), F32),
            pltpu.VMEM((hps, ts, LANES), BF16),
            pltpu.VMEM((N_GROUPS, 1, ts), F32),
            pltpu.VMEM((N_GROUPS, 1, ts), F32),
            pltpu.VMEM((N_GROUPS, 1, ts), F32),
            pltpu.VMEM((N_GROUPS, 1, ts), F32),
            pltpu.VMEM((N_LEVELS, CHUNK, CHUNK), F32),
            pltpu.VMEM((n_slots, CHUNK, LANES), F32),
            pltpu.VMEM((n_slots, 2 * CHUNK, LANES), BF16),
            pltpu.VMEM((n_slots, LANES, CHUNK), BF16),
            pltpu.VMEM((n_slots, CHUNK, CHUNK), BF16),
            pltpu.VMEM((n_slots, 1, LANES), F32),
            pltpu.VMEM((hps, ts, LANES), F32),
            pltpu.VMEM((N_GROUPS, LANES, LANES), F32),
        ],
        compiler_params=pltpu.CompilerParams(
            dimension_semantics=("arbitrary",),
            vmem_limit_bytes=VMEM_LIMIT),
        name="delta",
    )(proj3, proj3, proj3, proj3, proj3, conv_w4, alog_rep, dtb_rep, norm_row)


def _mlp_kernel(x_ref, val_ref, gate_ref, bv_ref, bg_ref, dw_ref, cb_ref, lg_ref, lb_ref,
                dn_ref, wout_ref, n2_ref, wup_ref, wdn_ref, fn_ref,
                o_ref, r_scr, h_scr, halo_scr, conv_scr, *, tm, rt, blocks_per_seq):
    i = pl.program_id(0)
    f = pl.program_id(1)
    n_blocks = pl.num_programs(0) - 1
    first_tap = CONF_HALO - (CONV_WIDTH - 1)

    def conf_part():
        seq_start = (i % blocks_per_seq) == 0
        h_scr[0:CONF_HALO, :] = jnp.where(seq_start, 0.0, halo_scr[f])
        a = val_ref[0] + bv_ref[0]
        g = gate_ref[0] + bg_ref[0]
        glu = a * _sigmoid(g)
        h_scr[CONF_HALO:CONF_HALO + tm, :] = glu
        zero_glu = _tile_zero(glu)
        halo_scr[f] = h_scr[tm:tm + CONF_HALO, :]
        slot = i % 2
        zeros = []
        for r in range(tm // rt):
            base = r * rt
            acc = dw_ref[0, 0:1, :] * h_scr[base + first_tap:base + first_tap + rt, :]
            for k in range(1, CONV_WIDTH):
                lo = base + first_tap + k
                acc = acc + dw_ref[0, k:k + 1, :] * h_scr[lo:lo + rt, :]
            y = acc + cb_ref[0]
            mu = jnp.mean(y, axis=-1, keepdims=True)
            cen = y - mu
            var = jnp.mean(cen * cen, axis=-1, keepdims=True)
            hn = cen * lax.rsqrt(var + EPS) * lg_ref[0] + lb_ref[0]
            out = _silu(hn)
            conv_scr[slot, f, base:base + rt, :] = out.astype(BF16)
            zeros.append(_tile_zero(out))
        return zero_glu, zeros

    def mlp_head():
        @pl.when(f == 0)
        def _():
            slot = (i + 1) % 2
            mix = jnp.concatenate([conv_scr[slot, c] for c in range(N_GROUPS)]
                                  + [dn_ref[c] for c in range(N_GROUPS)], axis=-1)
            h1 = x_ref[...] + jnp.dot(mix, wout_ref[...], preferred_element_type=F32)
            o_ref[...] = h1
            ms = jnp.mean(h1 * h1, axis=-1, keepdims=True)
            r_scr[...] = (h1 * lax.rsqrt(ms + EPS) * n2_ref[...]).astype(BF16)

    def mlp_body(zero_glu=None, zeros=None):
        m = jnp.dot(r_scr[...], wup_ref[...], preferred_element_type=F32)
        n_up = 0
        if zeros is not None:
            up_slab = 2 * LANES
            n_up = m.shape[1] // up_slab
            up_ties = [zero_glu] + zeros[:n_up - 1]
            m = jnp.concatenate(
                [jnp.concatenate([_add_tile(m[:, c * up_slab:c * up_slab + LANES], up_ties[c]),
                                  m[:, c * up_slab + LANES:(c + 1) * up_slab]], axis=1)
                 for c in range(n_up)], axis=1)
        act = jnp.square(jnp.maximum(m, 0.0)).astype(BF16)
        res = jnp.dot(act, wdn_ref[...], preferred_element_type=F32)
        if zeros is None:
            o_ref[...] += res
            return
        slab = o_ref.shape[1] // len(zeros)
        first = n_up - 1
        o_ref[:, 0:first * slab] += res[:, 0:first * slab]
        for n in range(first, len(zeros)):
            c0 = n * slab
            o_ref[:, c0 + LANES:c0 + slab] += res[:, c0 + LANES:c0 + slab]
            o_ref[:, c0:c0 + LANES] += _add_tile(res[:, c0:c0 + LANES], zeros[n])

    def mlp_tail():
        @pl.when(f == pl.num_programs(1) - 1)
        def _():
            h2 = o_ref[...]
            ms = jnp.mean(h2 * h2, axis=-1, keepdims=True)
            o_ref[...] = h2 * lax.rsqrt(ms + EPS) * fn_ref[...]

    @pl.when(i == 0)
    def _():
        @pl.when(f == 0)
        def _():
            halo_scr[...] = jnp.zeros_like(halo_scr)
        conf_part()

    @pl.when(jnp.logical_and(i > 0, i < n_blocks))
    def _():
        mlp_head()
        mlp_body(*conf_part())
        mlp_tail()

    @pl.when(i == n_blocks)
    def _():
        mlp_head()
        mlp_body()
        mlp_tail()


def _mlp(x2, proj3, b_glu, dw_w, dw_b, ln_g, ln_b, dn_o, w_out_b, n2, w_up_b, w_dn_b, fn,
         *, seq, tm=512, rt=64):
    t, d = x2.shape
    dff = w_up_b.shape[1]
    nb = t // tm
    nf = N_GROUPS
    tf = dff // nf
    prev = lambda i: jnp.maximum(i - 1, 0)
    nxt = lambda i: jnp.minimum(i, nb - 1)
    vec = lambda off: pl.BlockSpec((1, 1, LANES), lambda i, f: (off + f, 0, 0))
    return pl.pallas_call(
        functools.partial(_mlp_kernel, tm=tm, rt=rt, blocks_per_seq=seq // tm),
        out_shape=jax.ShapeDtypeStruct((t, d), F32),
        grid=(nb + 1, nf),
        in_specs=[
            pl.BlockSpec((tm, d), lambda i, f: (prev(i), 0)),
            pl.BlockSpec((1, tm, LANES), lambda i, f: (CB_VAL + f, nxt(i), 0)),
            pl.BlockSpec((1, tm, LANES), lambda i, f: (CB_GATE + f, nxt(i), 0)),
            vec(0), vec(N_GROUPS),
            pl.BlockSpec((1, CONV_WIDTH, LANES), lambda i, f: (f, 0, 0)),
            vec(0), vec(0), vec(0),
            pl.BlockSpec((N_GROUPS, tm, LANES), lambda i, f: (0, prev(i), 0)),
            pl.BlockSpec((d, d), lambda i, f: (0, 0), pipeline_mode=pl.Buffered(1)),
            pl.BlockSpec((1, d), lambda i, f: (0, 0)),
            pl.BlockSpec((d, tf), lambda i, f: (0, jnp.where(i == 0, 0, f))),
            pl.BlockSpec((tf, d), lambda i, f: (jnp.where(i == 0, 0, f), 0)),
            pl.BlockSpec((1, d), lambda i, f: (0, 0)),
        ],
        out_specs=pl.BlockSpec((tm, d), lambda i, f: (prev(i), 0)),
        scratch_shapes=[
            pltpu.VMEM((tm, d), BF16),
            pltpu.VMEM((CONF_HALO + tm, LANES), F32),
            pltpu.VMEM((N_GROUPS, CONF_HALO, LANES), F32),
            pltpu.VMEM((2, N_GROUPS, tm, LANES), BF16),
        ],
        compiler_params=pltpu.CompilerParams(
            dimension_semantics=("arbitrary", "arbitrary"),
            vmem_limit_bytes=VMEM_LIMIT),
        name="mlp",
    )(x2, proj3, proj3, b_glu, b_glu, dw_w, dw_b, ln_g, ln_b, dn_o,
      w_out_b, n2, w_up_b, w_dn_b, fn)


def _lane_rep(vals):
    return jnp.broadcast_to(vals.astype(F32)[:, None], (vals.shape[0], LANES))


def kernel(x, norm1_w, w_in, b_glu, conf_dw_w, conf_dw_b, conf_ln_g, conf_ln_b, dn_conv_w,
           dn_a_log, dn_dt_bias, dn_norm_w, w_out, norm2_w, w_mlp_up, w_mlp_down, final_norm_w):
    batch, seq, d = x.shape
    assert norm1_w.shape[0] == 1, "single-layer block"
    h = x.reshape(batch * seq, d)

    d_in = w_in.shape[2]
    w_in_p = jnp.pad(w_in[0].astype(BF16), ((0, 0), (0, N_COLBLOCKS * LANES - d_in)))
    proj3 = _inproj(h, norm1_w[0].reshape(1, d), w_in_p)

    conv_w4 = dn_conv_w[0].reshape(SHORT_CONV_WIDTH, 3, N_GROUPS, LANES)
    dn_o = _delta(proj3, conv_w4, _lane_rep(dn_a_log[0]), _lane_rep(dn_dt_bias[0]),
                  dn_norm_w[0].reshape(1, LANES), batch=batch, seq=seq)

    grp = lambda v: v.reshape(-1, 1, LANES)
    dw_w = conf_dw_w[0].reshape(CONV_WIDTH, N_GROUPS, LANES).transpose(1, 0, 2)
    out = _mlp(h, proj3, grp(b_glu[0]), dw_w, grp(conf_dw_b[0]), grp(conf_ln_g[0]),
               grp(conf_ln_b[0]), dn_o, w_out[0].astype(BF16), norm2_w[0].reshape(1, d),
               w_mlp_up[0].astype(BF16), w_mlp_down[0].astype(BF16),
               final_norm_w.reshape(1, d), seq=seq)
    return out.reshape(batch, seq, d)
```

```python
import functools

import jax
import jax.numpy as jnp
from jax import lax
from jax.experimental import pallas as pl
from jax.experimental.pallas import tpu as pltpu

F32 = jnp.float32
BF16 = jnp.bfloat16

EPS = 1e-6
LANES = 128
SUBLANES = 8
CONV_WIDTH = 31
SHORT_CONV_WIDTH = 4
N_GROUPS = 8
CONF_HALO = 32
DN_HALO = SUBLANES
CHUNK = 128
N_LEVELS = CHUNK.bit_length() - 1
HEADS_PER_STEP = 2

R_VAL, R_GATE, R_Q, R_K, R_V, R_Z = range(6)
N_ROLES = 6

VMEM_LIMIT = 56 * 1024 * 1024


def _sigmoid(x):
    return 0.5 * jnp.tanh(0.5 * x) + 0.5


def _silu(x):
    return x * _sigmoid(x)


def _derived_zero(x):
    bits = lax.bitcast_convert_type(x, jnp.uint32)
    bits = lax.shift_right_logical(lax.shift_right_logical(bits, jnp.uint32(16)), jnp.uint32(16))
    return lax.bitcast_convert_type(bits, F32)


def _tile_zero(x):
    return _derived_zero(jnp.max(x.reshape(-1, SUBLANES, x.shape[-1]), axis=0))


def _add_tile(x, tile):
    n = tile.shape[0]
    return jnp.concatenate([x[0:n] + tile, x[n:]], axis=0)


def _softplus(x):
    return jnp.maximum(x, 0.0) + jnp.log1p(jnp.exp(-jnp.abs(x)))


def _inproj_kernel(x_ref, nw_ref, w_ref, wba_ref, cw_ref, o_ref, ba_ref,
                   u_scr, st_scr, halo_scr, *, tm, blocks_per_seq):
    i = pl.program_id(0)
    j = pl.program_id(1)
    n_pairs = pl.num_programs(1) - 1
    back = SHORT_CONV_WIDTH - 1

    @pl.when(j == 0)
    def _():
        x = x_ref[...]
        ms = jnp.mean(x * x, axis=-1, keepdims=True)
        u_scr[...] = (x * lax.rsqrt(ms + EPS) * nw_ref[...]).astype(BF16)

    @pl.when(jnp.logical_and(i == 0, j == 0))
    def _():
        halo_scr[...] = jnp.zeros_like(halo_scr)

    @pl.when(j < n_pairs)
    def _():
        res = jnp.dot(u_scr[...], w_ref[...], preferred_element_type=F32)
        seq_start = (i % blocks_per_seq) == 0
        for c in range(N_ROLES * HEADS_PER_STEP):
            role, hh = divmod(c, HEADS_PER_STEP)
            blk = res[:, c * LANES:(c + 1) * LANES]
            if role in (R_Q, R_K, R_V):
                r3 = role - R_Q
                h = HEADS_PER_STEP * j + hh
                slot = c % st_scr.shape[0]
                st_scr[slot, 0:DN_HALO, :] = jnp.where(seq_start, 0.0, halo_scr[h, r3])
                st_scr[slot, DN_HALO:DN_HALO + tm, :] = blk
                halo_scr[h, r3] = blk[tm - DN_HALO:tm, :]
                lo = DN_HALO - back
                hv = (0.5 * cw_ref[0, r3, pl.ds(h, 1), :]) * st_scr[slot, lo:lo + tm, :]
                for m in range(1, SHORT_CONV_WIDTH):
                    hv = hv + (0.5 * cw_ref[m, r3, pl.ds(h, 1), :]) * st_scr[slot, lo + m:lo + m + tm, :]
                y = hv + hv * jnp.tanh(hv)
                if role == R_Q:
                    y = y * (lax.rsqrt(jnp.sum(y * y, axis=-1, keepdims=True) + EPS)
                             * (LANES ** -0.5))
                elif role == R_K:
                    y = y * lax.rsqrt(jnp.sum(y * y, axis=-1, keepdims=True) + EPS)
                blk = y
            o_ref[role, hh] = blk

    @pl.when(j == n_pairs)
    def _():
        ba_ref[0] = jnp.dot(u_scr[...], wba_ref[...], preferred_element_type=F32)


def _inproj(x2, norm_w, w_pairs, w_ba, conv_w4, *, seq, tm=1024):
    t, d = x2.shape
    hps = HEADS_PER_STEP
    pairs = N_GROUPS // hps
    tn = N_ROLES * hps * LANES
    last = pairs - 1
    return pl.pallas_call(
        functools.partial(_inproj_kernel, tm=tm, blocks_per_seq=seq // tm),
        out_shape=(jax.ShapeDtypeStruct((N_ROLES, N_GROUPS, t, LANES), F32),
                   jax.ShapeDtypeStruct((1, t, LANES), F32)),
        grid=(t // tm, pairs + 1),
        in_specs=[
            pl.BlockSpec((tm, d), lambda i, j: (i, 0)),
            pl.BlockSpec((1, d), lambda i, j: (0, 0)),
            pl.BlockSpec((d, tn), lambda i, j: (0, jnp.minimum(j, last))),
            pl.BlockSpec((d, LANES), lambda i, j: (0, 0)),
            pl.BlockSpec((SHORT_CONV_WIDTH, 3, N_GROUPS, LANES), lambda i, j: (0, 0, 0, 0)),
        ],
        out_specs=(
            pl.BlockSpec((N_ROLES, hps, tm, LANES), lambda i, j: (0, jnp.minimum(j, last), i, 0)),
            pl.BlockSpec((1, tm, LANES), lambda i, j: (0, i, 0)),
        ),
        scratch_shapes=[
            pltpu.VMEM((tm, d), BF16),
            pltpu.VMEM((2, DN_HALO + tm, LANES), F32),
            pltpu.VMEM((N_GROUPS, 3, DN_HALO, LANES), F32),
        ],
        compiler_params=pltpu.CompilerParams(
            dimension_semantics=("arbitrary", "arbitrary"),
            vmem_limit_bytes=VMEM_LIMIT),
        name="inproj",
    )(x2, norm_w, w_pairs, w_ba, conv_w4)


def _delta_kernel(q_ref, k_ref, v_ref, z_ref, ba_ref, alog_ref, dtb_ref, nw_ref,
                  o_ref,
                  beta_scr, cs_scr, f_scr, ld_scr, mask_scr,
                  u_scr, wq_scr, kgt_scr, intra_scr, ldc_scr, o_scr, s_scr,
                  *, ts, group, rt, n_items, blocks_per_seq):
    step = pl.program_id(0)
    item = jnp.minimum(step, n_items - 1)
    prev = jnp.maximum(step - 1, 0)
    pairs = N_GROUPS // HEADS_PER_STEP
    hp = item % pairs
    i = (item // pairs) % blocks_per_seq
    heads = [HEADS_PER_STEP * hp + hh for hh in range(HEADS_PER_STEP)]
    heads_prev = [HEADS_PER_STEP * (prev % pairs) + hh for hh in range(HEADS_PER_STEP)]
    nc = ts // CHUNK

    r_io = lax.broadcasted_iota(jnp.int32, (CHUNK, CHUNK), 0)
    c_io = lax.broadcasted_iota(jnp.int32, (CHUNK, CHUNK), 1)
    causal = r_io >= c_io
    eye = jnp.where(r_io == c_io, 1.0, 0.0).astype(F32)

    @pl.when(hp == 0)
    def _():
        upper = jnp.where(r_io <= c_io, 1.0, 0.0).astype(F32)
        ones = jnp.ones((CHUNK, CHUNK), F32)
        for lvl in range(N_LEVELS):
            same = (r_io >> (lvl + 1)) == (c_io >> (lvl + 1))
            lower_left = jnp.where(((r_io >> lvl) & 1) == 1, 1.0, 0.0) * \
                jnp.where(((c_io >> lvl) & 1) == 0, 1.0, 0.0)
            mask_scr[lvl] = jnp.where(same, lower_left, 0.0).astype(BF16)
        ba_t = ba_ref[0].T
        beta = _sigmoid(ba_t[0:N_GROUPS, :])
        a_raw = ba_t[N_GROUPS:2 * N_GROUPS, :]
        neg_a = -jnp.exp(alog_ref[...])
        for c in range(nc):
            cols = slice(c * CHUNK, (c + 1) * CHUNK)
            g_c = neg_a * _softplus(a_raw[:, cols] + dtb_ref[...])
            cs = jnp.dot(g_c, upper, precision=lax.Precision.HIGHEST,
                         preferred_element_type=F32)
            tot = jnp.dot(g_c, ones, precision=lax.Precision.HIGHEST,
                          preferred_element_type=F32)
            f_c = jnp.exp(tot - cs) * beta[:, cols]
            ld_c = jnp.exp(tot)
            for hh in range(N_GROUPS):
                one = slice(hh, hh + 1)
                beta_scr[hh, :, cols] = beta[one, cols]
                cs_scr[hh, :, cols] = cs[one, :]
                f_scr[hh, :, cols] = f_c[one, :]
                ld_scr[hh, :, cols] = ld_c[one, :]

    @pl.when(i == 0)
    def _():
        for h in heads:
            s_scr[h] = jnp.zeros((LANES, LANES), F32)

    set_size = HEADS_PER_STEP * group

    @pl.when(step == 0)
    def _():
        late = slice(set_size, 2 * set_size)
        u_scr[late] = jnp.zeros((set_size,) + u_scr.shape[1:], F32)
        wq_scr[late] = jnp.zeros((set_size,) + wq_scr.shape[1:], BF16)
        kgt_scr[late] = jnp.zeros((set_size,) + kgt_scr.shape[1:], BF16)
        intra_scr[late] = jnp.zeros((set_size,) + intra_scr.shape[1:], BF16)
        ldc_scr[late] = jnp.zeros((set_size,) + ldc_scr.shape[1:], F32)
        o_scr[...] = jnp.zeros_like(o_scr)

    n_tiles = ts // rt

    def prep_group(first_chunk, slot0):
        items = []
        for hh, h in enumerate(heads):
            for g in range(group):
                slot = slot0 + hh * group + g
                r0 = pl.multiple_of((first_chunk + g) * CHUNK, CHUNK)
                rows = pl.ds(r0, CHUNK)
                k_c = k_ref[0, hh, rows, :]
                qs_c = q_ref[0, hh, rows, :]
                cs_r = jnp.broadcast_to(cs_scr[h, :, rows], (CHUNK, CHUNK))
                beta_r = jnp.broadcast_to(beta_scr[h, :, rows], (CHUNK, CHUNK))
                cs_c = cs_r.T
                db = jnp.exp(jnp.where(causal, cs_c - cs_r, -jnp.inf)) * beta_r
                kq = lax.dot_general(
                    jnp.concatenate([k_c, qs_c], axis=0).astype(BF16), k_c.astype(BF16),
                    (((1,), (1,)), ((), ())), preferred_element_type=F32)
                a_mat = kq[:CHUNK] * db
                intra_scr[slot] = (kq[CHUNK:] * db).astype(BF16)
                eg = jnp.exp(cs_c)
                wq_scr[slot, CHUNK:2 * CHUNK, :] = (qs_c * eg).astype(BF16)
                kgt_scr[slot] = (k_c.T * f_scr[h, :, rows]).astype(BF16)
                ldc_scr[slot] = ld_scr[h, :, pl.ds(r0, LANES)]
                rhs = jnp.concatenate([v_ref[0, hh, rows, :].astype(BF16),
                                       (k_c * eg).astype(BF16)], axis=1)
                items.append(dict(a=a_mat.astype(BF16), x=eye - a_mat * mask_scr[0].astype(F32),
                                  rhs=rhs, slot=slot))
        yield
        for lvl in range(1, N_LEVELS):
            for it in items:
                it["xb"] = it["x"].astype(BF16)
                it["mx"] = jnp.dot(it["a"] * mask_scr[lvl], it["xb"], preferred_element_type=F32)
            yield
            for it in items:
                it["x"] = it["x"] - jnp.dot(it["xb"], it["mx"].astype(BF16),
                                            preferred_element_type=F32)
            yield
        for it in items:
            uw = jnp.dot(it["x"].astype(BF16), it["rhs"], preferred_element_type=F32)
            u_scr[it["slot"]] = uw[:, :LANES]
            wq_scr[it["slot"], 0:CHUNK, :] = uw[:, LANES:].astype(BF16)
        yield

    def step_group(first_chunk, slot0, states, ties=()):
        for g in range(group):
            r0 = pl.multiple_of((first_chunk + g) * CHUNK, CHUNK)
            rows = pl.ds(r0, CHUNK)
            ws = [jnp.dot(wq_scr[slot0 + hh * group + g], states[hh].astype(BF16),
                          preferred_element_type=F32) for hh in range(HEADS_PER_STEP)]
            yield
            for hh in range(HEADS_PER_STEP):
                slot = slot0 + hh * group + g
                v_new = (u_scr[slot] - ws[hh][:CHUNK]).astype(BF16)
                o_c = ws[hh][CHUNK:] + jnp.dot(intra_scr[slot], v_new,
                                               preferred_element_type=F32)
                if ties:
                    o_c = _add_tile(o_c, ties.pop())
                o_scr[hh, rows, :] = o_c
                states[hh] = states[hh] * ldc_scr[slot] + jnp.dot(
                    kgt_scr[slot], v_new, preferred_element_type=F32)
            yield

    def interleave(*gens):
        live = list(gens)
        while live:
            for gen in list(live):
                try:
                    next(gen)
                except StopIteration:
                    live.remove(gen)

    assert nc == 2 * group and n_tiles % 2 == 0

    def out_rows(rows):
        tops = []
        for hh in range(HEADS_PER_STEP):
            o = o_scr[hh, rows, :]
            o = o * lax.rsqrt(jnp.mean(o * o, axis=-1, keepdims=True) + EPS) * nw_ref[...]
            res = o * _silu(z_ref[0, hh, rows, :])
            o_ref[hh, rows, :] = res.astype(o_ref.dtype)
            tops.append(res)
        return _tile_zero(functools.reduce(jnp.maximum, tops))

    def out_tiles(ties):
        for r in range(n_tiles):
            ties.append(out_rows(pl.ds(r * rt, rt)))
            for _ in range(2 * group // n_tiles):
                yield

    states = [s_scr[h] for h in heads_prev]
    interleave(prep_group(0, 0), step_group(group, set_size, states))
    for hh, h in enumerate(heads_prev):
        s_scr[h] = states[hh]

    out_ties = []
    states = [s_scr[h] for h in heads]
    interleave(out_tiles(out_ties), step_group(0, 0, states, out_ties),
               prep_group(group, set_size))
    for hh, h in enumerate(heads):
        s_scr[h] = states[hh]


def _delta(proj4, ba3, alog_rep, dtb_rep, norm_row, *, batch, seq, ts=2048, group=8, rt=256):
    t = proj4.shape[2]
    ns = seq // ts
    hps = HEADS_PER_STEP
    pairs = N_GROUPS // hps
    n_items = batch * ns * pairs
    cur = lambda s: jnp.minimum(s, n_items - 1)
    prv = lambda s: jnp.maximum(s - 1, 0)

    def slab(role, which):
        return pl.BlockSpec((1, hps, ts, LANES),
                            lambda s: (role, which(s) % pairs, which(s) // pairs, 0))

    rep = pl.BlockSpec((N_GROUPS, LANES), lambda s: (0, 0))
    n_slots = 2 * hps * group
    return pl.pallas_call(
        functools.partial(_delta_kernel, ts=ts, group=group, rt=rt, n_items=n_items,
                          blocks_per_seq=ns),
        out_shape=jax.ShapeDtypeStruct((N_GROUPS, t, LANES), BF16),
        grid=(n_items + 1,),
        in_specs=[
            slab(R_Q, cur), slab(R_K, cur), slab(R_V, cur), slab(R_Z, prv),
            pl.BlockSpec((1, ts, LANES), lambda s: (0, cur(s) // pairs, 0)),
            rep, rep,
            pl.BlockSpec((1, LANES), lambda s: (0, 0)),
        ],
        out_specs=pl.BlockSpec((hps, ts, LANES), lambda s: (prv(s) % pairs, prv(s) // pairs, 0)),
        scratch_shapes=[
            pltpu.VMEM((N_GROUPS, 1, ts), F32),
            pltpu.VMEM((N_GROUPS, 1, ts), F32),
            pltpu.VMEM((N_GROUPS, 1, ts), F32),
            pltpu.VMEM((N_GROUPS, 1, ts), F32),
            pltpu.VMEM((N_LEVELS, CHUNK, CHUNK), BF16),
            pltpu.VMEM((n_slots, CHUNK, LANES), F32),
            pltpu.VMEM((n_slots, 2 * CHUNK, LANES), BF16),
            pltpu.VMEM((n_slots, LANES, CHUNK), BF16),
            pltpu.VMEM((n_slots, CHUNK, CHUNK), BF16),
            pltpu.VMEM((n_slots, 1, LANES), F32),
            pltpu.VMEM((hps, ts, LANES), F32),
            pltpu.VMEM((N_GROUPS, LANES, LANES), F32),
        ],
        compiler_params=pltpu.CompilerParams(
            dimension_semantics=("arbitrary",),
            vmem_limit_bytes=VMEM_LIMIT),
        name="delta",
    )(proj4, proj4, proj4, proj4, ba3, alog_rep, dtb_rep, norm_row)


def _mlp_kernel(x_ref, val_ref, gate_ref, bv_ref, bg_ref, dw_ref, cb_ref, lg_ref, lb_ref,
                dn_ref, wout_ref, n2_ref, wup_ref, wdn_ref, fn_ref,
                o_ref, r_scr, h_scr, halo_scr, conv_scr, *, tm, rt, blocks_per_seq):
    i = pl.program_id(0)
    f = pl.program_id(1)
    n_blocks = pl.num_programs(0) - 1
    first_tap = CONF_HALO - (CONV_WIDTH - 1)

    def conf_part():
        seq_start = (i % blocks_per_seq) == 0
        h_scr[0:CONF_HALO, :] = jnp.where(seq_start, 0.0, halo_scr[f])
        a = val_ref[0, 0] + bv_ref[0]
        g = gate_ref[0, 0] + bg_ref[0]
        glu = a * _sigmoid(g)
        h_scr[CONF_HALO:CONF_HALO + tm, :] = glu
        zero_glu = _tile_zero(glu)
        halo_scr[f] = h_scr[tm:tm + CONF_HALO, :]
        slot = i % 2
        zeros = []
        for r in range(tm // rt):
            base = r * rt
            acc = dw_ref[0, 0:1, :] * h_scr[base + first_tap:base + first_tap + rt, :]
            for k in range(1, CONV_WIDTH):
                lo = base + first_tap + k
                acc = acc + dw_ref[0, k:k + 1, :] * h_scr[lo:lo + rt, :]
            y = acc + cb_ref[0]
            mu = jnp.mean(y, axis=-1, keepdims=True)
            cen = y - mu
            var = jnp.mean(cen * cen, axis=-1, keepdims=True)
            hn = cen * lax.rsqrt(var + EPS) * lg_ref[0] + lb_ref[0]
            out = _silu(hn)
            conv_scr[slot, f, base:base + rt, :] = out.astype(BF16)
            zeros.append(_tile_zero(out))
        return zero_glu, zeros

    def mlp_head():
        @pl.when(f == 0)
        def _():
            slot = (i + 1) % 2
            mix = jnp.concatenate([conv_scr[slot, c] for c in range(N_GROUPS)]
                                  + [dn_ref[c] for c in range(N_GROUPS)], axis=-1)
            h1 = x_ref[...] + jnp.dot(mix, wout_ref[...], preferred_element_type=F32)
            o_ref[...] = h1
            ms = jnp.mean(h1 * h1, axis=-1, keepdims=True)
            r_scr[...] = (h1 * lax.rsqrt(ms + EPS) * n2_ref[...]).astype(BF16)

    def mlp_body(zero_glu=None, zeros=None):
        m = jnp.dot(r_scr[...], wup_ref[...], preferred_element_type=F32)
        n_up = 0
        if zeros is not None:
            up_slab = 2 * LANES
            n_up = m.shape[1] // up_slab
            up_ties = [zero_glu] + zeros[:n_up - 1]
            m = jnp.concatenate(
                [jnp.concatenate([_add_tile(m[:, c * up_slab:c * up_slab + LANES], up_ties[c]),
                                  m[:, c * up_slab + LANES:(c + 1) * up_slab]], axis=1)
                 for c in range(n_up)], axis=1)
        act = jnp.square(jnp.maximum(m, 0.0)).astype(BF16)
        res = jnp.dot(act, wdn_ref[...], preferred_element_type=F32)
        if zeros is None:
            o_ref[...] += res
            return
        slab = o_ref.shape[1] // len(zeros)
        first = n_up - 1
        o_ref[:, 0:first * slab] += res[:, 0:first * slab]
        for n in range(first, len(zeros)):
            c0 = n * slab
            o_ref[:, c0 + LANES:c0 + slab] += res[:, c0 + LANES:c0 + slab]
            o_ref[:, c0:c0 + LANES] += _add_tile(res[:, c0:c0 + LANES], zeros[n])

    def mlp_tail():
        @pl.when(f == pl.num_programs(1) - 1)
        def _():
            h2 = o_ref[...]
            ms = jnp.mean(h2 * h2, axis=-1, keepdims=True)
            o_ref[...] = h2 * lax.rsqrt(ms + EPS) * fn_ref[...]

    @pl.when(i == 0)
    def _():
        @pl.when(f == 0)
        def _():
            halo_scr[...] = jnp.zeros_like(halo_scr)
        conf_part()

    @pl.when(jnp.logical_and(i > 0, i < n_blocks))
    def _():
        mlp_head()
        mlp_body(*conf_part())
        mlp_tail()

    @pl.when(i == n_blocks)
    def _():
        mlp_head()
        mlp_body()
        mlp_tail()


def _mlp(x2, proj4, b_glu, dw_w, dw_b, ln_g, ln_b, dn_o, w_out_b, n2, w_up_b, w_dn_b, fn,
         *, seq, tm=512, rt=64):
    t, d = x2.shape
    dff = w_up_b.shape[1]
    nb = t // tm
    nf = N_GROUPS
    tf = dff // nf
    prev = lambda i: jnp.maximum(i - 1, 0)
    nxt = lambda i: jnp.minimum(i, nb - 1)
    vec = lambda off: pl.BlockSpec((1, 1, LANES), lambda i, f: (off + f, 0, 0))
    return pl.pallas_call(
        functools.partial(_mlp_kernel, tm=tm, rt=rt, blocks_per_seq=seq // tm),
        out_shape=jax.ShapeDtypeStruct((t, d), F32),
        grid=(nb + 1, nf),
        in_specs=[
            pl.BlockSpec((tm, d), lambda i, f: (prev(i), 0)),
            pl.BlockSpec((1, 1, tm, LANES), lambda i, f: (R_VAL, f, nxt(i), 0)),
            pl.BlockSpec((1, 1, tm, LANES), lambda i, f: (R_GATE, f, nxt(i), 0)),
            vec(0), vec(N_GROUPS),
            pl.BlockSpec((1, CONV_WIDTH, LANES), lambda i, f: (f, 0, 0)),
            vec(0), vec(0), vec(0),
            pl.BlockSpec((N_GROUPS, tm, LANES), lambda i, f: (0, prev(i), 0)),
            pl.BlockSpec((d, d), lambda i, f: (0, 0), pipeline_mode=pl.Buffered(1)),
            pl.BlockSpec((1, d), lambda i, f: (0, 0)),
            pl.BlockSpec((d, tf), lambda i, f: (0, jnp.where(i == 0, 0, f))),
            pl.BlockSpec((tf, d), lambda i, f: (jnp.where(i == 0, 0, f), 0)),
            pl.BlockSpec((1, d), lambda i, f: (0, 0)),
        ],
        out_specs=pl.BlockSpec((tm, d), lambda i, f: (prev(i), 0)),
        scratch_shapes=[
            pltpu.VMEM((tm, d), BF16),
            pltpu.VMEM((CONF_HALO + tm, LANES), F32),
            pltpu.VMEM((N_GROUPS, CONF_HALO, LANES), F32),
            pltpu.VMEM((2, N_GROUPS, tm, LANES), BF16),
        ],
        compiler_params=pltpu.CompilerParams(
            dimension_semantics=("arbitrary", "arbitrary"),
            vmem_limit_bytes=VMEM_LIMIT),
        name="mlp",
    )(x2, proj4, proj4, b_glu, b_glu, dw_w, dw_b, ln_g, ln_b, dn_o,
      w_out_b, n2, w_up_b, w_dn_b, fn)


def _lane_rep(vals):
    return jnp.broadcast_to(vals.astype(F32)[:, None], (vals.shape[0], LANES))


def kernel(x, norm1_w, w_in, b_glu, conf_dw_w, conf_dw_b, conf_ln_g, conf_ln_b, dn_conv_w,
           dn_a_log, dn_dt_bias, dn_norm_w, w_out, norm2_w, w_mlp_up, w_mlp_down, final_norm_w):
    batch, seq, d = x.shape
    assert norm1_w.shape[0] == 1, "single-layer block"
    h = x.reshape(batch * seq, d)

    n_main = N_ROLES * N_GROUPS * LANES
    pairs = N_GROUPS // HEADS_PER_STEP
    w_main = w_in[0][:, :n_main].astype(BF16).reshape(d, N_ROLES, pairs, HEADS_PER_STEP, LANES)
    w_pairs = w_main.transpose(0, 2, 1, 3, 4).reshape(d, n_main)
    w_ba = jnp.pad(w_in[0][:, n_main:].astype(BF16),
                   ((0, 0), (0, LANES - (w_in.shape[2] - n_main))))
    conv_w4 = dn_conv_w[0].reshape(SHORT_CONV_WIDTH, 3, N_GROUPS, LANES)
    proj4, ba3 = _inproj(h, norm1_w[0].reshape(1, d), w_pairs, w_ba, conv_w4, seq=seq)

    dn_o = _delta(proj4, ba3, _lane_rep(dn_a_log[0]), _lane_rep(dn_dt_bias[0]),
                  dn_norm_w[0].reshape(1, LANES), batch=batch, seq=seq)

    grp = lambda v: v.reshape(-1, 1, LANES)
    dw_w = conf_dw_w[0].reshape(CONV_WIDTH, N_GROUPS, LANES).transpose(1, 0, 2)
    out = _mlp(h, proj4, grp(b_glu[0]), dw_w, grp(conf_dw_b[0]), grp(conf_ln_g[0]),
               grp(conf_ln_b[0]), dn_o, w_out[0].astype(BF16), norm2_w[0].reshape(1, d),
               w_mlp_up[0].astype(BF16), w_mlp_down[0].astype(BF16),
               final_norm_w.reshape(1, d), seq=seq)
    return out.reshape(batch, seq, d)
```

```python
import functools

import jax
import jax.numpy as jnp
from jax import lax
from jax.experimental import pallas as pl
from jax.experimental.pallas import tpu as pltpu

F32 = jnp.float32
BF16 = jnp.bfloat16

EPS = 1e-6
LANES = 128
SUBLANES = 8
CONV_WIDTH = 31
SHORT_CONV_WIDTH = 4
N_GROUPS = 8
CONF_HALO = 32
DN_HALO = SUBLANES
CHUNK = 128
N_LEVELS = CHUNK.bit_length() - 1
HEADS_PER_STEP = 2

R_Q, R_VAL, R_K, R_GATE, R_V, R_Z = range(6)
N_ROLES = 6
W_IN_ROLE_ORDER = (R_VAL, R_GATE, R_Q, R_K, R_V, R_Z)

VMEM_LIMIT = 56 * 1024 * 1024


def _sigmoid(x):
    return 0.5 * jnp.tanh(0.5 * x) + 0.5


def _silu(x):
    return x * _sigmoid(x)


def _derived_zero(x):
    bits = lax.bitcast_convert_type(x, jnp.uint32)
    bits = lax.shift_right_logical(lax.shift_right_logical(bits, jnp.uint32(16)), jnp.uint32(16))
    return lax.bitcast_convert_type(bits, F32)


def _tile_zero(x):
    return _derived_zero(jnp.max(x.reshape(-1, SUBLANES, x.shape[-1]), axis=0))


def _add_tile(x, tile):
    n = tile.shape[0]
    return jnp.concatenate([x[0:n] + tile, x[n:]], axis=0)


def _softplus(x):
    return jnp.maximum(x, 0.0) + jnp.log1p(jnp.exp(-jnp.abs(x)))


def _inproj_kernel(x_ref, nw_ref, w_ref, wba_ref, cw_ref, o_ref, ba_ref,
                   u_scr, st_scr, halo_scr, *, tm, rt, blocks_per_seq):
    i = pl.program_id(0)
    j = pl.program_id(1)
    n_pairs = pl.num_programs(1) - 1
    back = SHORT_CONV_WIDTH - 1

    @pl.when(j == 0)
    def _():
        x = x_ref[...]
        ms = jnp.mean(x * x, axis=-1, keepdims=True)
        u_scr[...] = (x * lax.rsqrt(ms + EPS) * nw_ref[...]).astype(BF16)

    @pl.when(jnp.logical_and(i == 0, j == 0))
    def _():
        halo_scr[...] = jnp.zeros_like(halo_scr)

    @pl.when(j < n_pairs)
    def _():
        res = jnp.dot(u_scr[...], w_ref[...], preferred_element_type=F32)
        seq_start = (i % blocks_per_seq) == 0
        for c in range(N_ROLES * HEADS_PER_STEP):
            role, hh = divmod(c, HEADS_PER_STEP)
            blk = res[:, c * LANES:(c + 1) * LANES]
            if role in (R_Q, R_K, R_V):
                r3 = (R_Q, R_K, R_V).index(role)
                h = HEADS_PER_STEP * j + hh
                slot = c % st_scr.shape[0]
                st_scr[slot, 0:DN_HALO, :] = jnp.where(seq_start, 0.0, halo_scr[h, r3])
                st_scr[slot, DN_HALO:DN_HALO + tm, :] = blk
                halo_scr[h, r3] = blk[tm - DN_HALO:tm, :]
                taps = [0.5 * cw_ref[m, r3, pl.ds(h, 1), :] for m in range(SHORT_CONV_WIDTH)]
                for r in range(tm // rt):
                    lo = DN_HALO - back + r * rt
                    hv = taps[0] * st_scr[slot, lo:lo + rt, :]
                    for m in range(1, SHORT_CONV_WIDTH):
                        hv = hv + taps[m] * st_scr[slot, lo + m:lo + m + rt, :]
                    y = hv + hv * jnp.tanh(hv)
                    if role == R_Q:
                        y = y * (lax.rsqrt(jnp.sum(y * y, axis=-1, keepdims=True) + EPS)
                                 * (LANES ** -0.5))
                    elif role == R_K:
                        y = y * lax.rsqrt(jnp.sum(y * y, axis=-1, keepdims=True) + EPS)
                    o_ref[role, hh, r * rt:(r + 1) * rt, :] = y.astype(o_ref.dtype)
            else:
                o_ref[role, hh] = blk.astype(o_ref.dtype)

    @pl.when(j == n_pairs)
    def _():
        ba_ref[0] = jnp.dot(u_scr[...], wba_ref[...], preferred_element_type=F32)


def _inproj(x2, norm_w, w_pairs, w_ba, conv_w4, *, seq, tm=1024, rt=128):
    t, d = x2.shape
    hps = HEADS_PER_STEP
    pairs = N_GROUPS // hps
    tn = N_ROLES * hps * LANES
    last = pairs - 1
    return pl.pallas_call(
        functools.partial(_inproj_kernel, tm=tm, rt=rt, blocks_per_seq=seq // tm),
        out_shape=(jax.ShapeDtypeStruct((N_ROLES, N_GROUPS, t, LANES), BF16),
                   jax.ShapeDtypeStruct((1, t, LANES), F32)),
        grid=(t // tm, pairs + 1),
        in_specs=[
            pl.BlockSpec((tm, d), lambda i, j: (i, 0)),
            pl.BlockSpec((1, d), lambda i, j: (0, 0)),
            pl.BlockSpec((d, tn), lambda i, j: (0, jnp.minimum(j, last))),
            pl.BlockSpec((d, LANES), lambda i, j: (0, 0)),
            pl.BlockSpec((SHORT_CONV_WIDTH, 3, N_GROUPS, LANES), lambda i, j: (0, 0, 0, 0)),
        ],
        out_specs=(
            pl.BlockSpec((N_ROLES, hps, tm, LANES), lambda i, j: (0, jnp.minimum(j, last), i, 0)),
            pl.BlockSpec((1, tm, LANES), lambda i, j: (0, i, 0)),
        ),
        scratch_shapes=[
            pltpu.VMEM((tm, d), BF16),
            pltpu.VMEM((2, DN_HALO + tm, LANES), F32),
            pltpu.VMEM((N_GROUPS, 3, DN_HALO, LANES), F32),
        ],
        compiler_params=pltpu.CompilerParams(
            dimension_semantics=("arbitrary", "arbitrary"),
            vmem_limit_bytes=VMEM_LIMIT),
        name="inproj",
    )(x2, norm_w, w_pairs, w_ba, conv_w4)


def _delta_kernel(q_ref, k_ref, v_ref, z_ref, ba_ref, alog_ref, dtb_ref, nw_ref,
                  o_ref,
                  beta_scr, cs_scr, f_scr, ld_scr, mask_scr,
                  u_scr, wq_scr, kgt_scr, intra_scr, ldc_scr, o_scr, s_scr,
                  *, ts, group, rt, n_items, blocks_per_seq):
    step = pl.program_id(0)
    item = jnp.minimum(step, n_items - 1)
    prev = jnp.maximum(step - 1, 0)
    pairs = N_GROUPS // HEADS_PER_STEP
    hp = item % pairs
    i = (item // pairs) % blocks_per_seq
    heads = [HEADS_PER_STEP * hp + hh for hh in range(HEADS_PER_STEP)]
    heads_prev = [HEADS_PER_STEP * (prev % pairs) + hh for hh in range(HEADS_PER_STEP)]
    nc = ts // CHUNK

    r_io = lax.broadcasted_iota(jnp.int32, (CHUNK, CHUNK), 0)
    c_io = lax.broadcasted_iota(jnp.int32, (CHUNK, CHUNK), 1)
    causal = r_io >= c_io
    eye = jnp.where(r_io == c_io, 1.0, 0.0).astype(F32)

    @pl.when(hp == 0)
    def _():
        upper = jnp.where(r_io <= c_io, 1.0, 0.0).astype(F32)
        ones = jnp.ones((CHUNK, CHUNK), F32)
        for lvl in range(N_LEVELS):
            same = (r_io >> (lvl + 1)) == (c_io >> (lvl + 1))
            lower_left = jnp.where(((r_io >> lvl) & 1) == 1, 1.0, 0.0) * \
                jnp.where(((c_io >> lvl) & 1) == 0, 1.0, 0.0)
            mask_scr[lvl] = jnp.where(same, lower_left, 0.0).astype(BF16)
        ba_t = ba_ref[0].T
        beta = _sigmoid(ba_t[0:N_GROUPS, :])
        a_raw = ba_t[N_GROUPS:2 * N_GROUPS, :]
        neg_a = -jnp.exp(alog_ref[...])
        for c in range(nc):
            cols = slice(c * CHUNK, (c + 1) * CHUNK)
            g_c = neg_a * _softplus(a_raw[:, cols] + dtb_ref[...])
            cs = jnp.dot(g_c, upper, precision=lax.Precision.HIGHEST,
                         preferred_element_type=F32)
            tot = jnp.dot(g_c, ones, precision=lax.Precision.HIGHEST,
                          preferred_element_type=F32)
            f_c = jnp.exp(tot - cs) * beta[:, cols]
            ld_c = jnp.exp(tot)
            for hh in range(N_GROUPS):
                one = slice(hh, hh + 1)
                beta_scr[hh, :, cols] = beta[one, cols]
                cs_scr[hh, :, cols] = cs[one, :]
                f_scr[hh, :, cols] = f_c[one, :]
                ld_scr[hh, :, cols] = ld_c[one, :]

    @pl.when(i == 0)
    def _():
        for h in heads:
            s_scr[h] = jnp.zeros((LANES, LANES), F32)

    set_size = HEADS_PER_STEP * group

    @pl.when(step == 0)
    def _():
        late = slice(set_size, 2 * set_size)
        u_scr[late] = jnp.zeros((set_size,) + u_scr.shape[1:], F32)
        wq_scr[late] = jnp.zeros((set_size,) + wq_scr.shape[1:], BF16)
        kgt_scr[late] = jnp.zeros((set_size,) + kgt_scr.shape[1:], BF16)
        intra_scr[late] = jnp.zeros((set_size,) + intra_scr.shape[1:], BF16)
        ldc_scr[late] = jnp.zeros((set_size,) + ldc_scr.shape[1:], F32)
        o_scr[...] = jnp.zeros_like(o_scr)

    n_tiles = ts // rt

    def prep_group(first_chunk, slot0):
        items = []
        for hh, h in enumerate(heads):
            for g in range(group):
                slot = slot0 + hh * group + g
                r0 = pl.multiple_of((first_chunk + g) * CHUNK, CHUNK)
                rows = pl.ds(r0, CHUNK)
                k_c = k_ref[0, hh, rows, :].astype(F32)
                qs_c = q_ref[0, hh, rows, :].astype(F32)
                cs_r = jnp.broadcast_to(cs_scr[h, :, rows], (CHUNK, CHUNK))
                beta_r = jnp.broadcast_to(beta_scr[h, :, rows], (CHUNK, CHUNK))
                cs_c = cs_r.T
                db = jnp.exp(jnp.where(causal, cs_c - cs_r, -jnp.inf)) * beta_r
                kq = lax.dot_general(
                    jnp.concatenate([k_c, qs_c], axis=0).astype(BF16), k_c.astype(BF16),
                    (((1,), (1,)), ((), ())), preferred_element_type=F32)
                a_mat = kq[:CHUNK] * db
                intra_scr[slot] = (kq[CHUNK:] * db).astype(BF16)
                eg = jnp.exp(cs_c)
                wq_scr[slot, CHUNK:2 * CHUNK, :] = (qs_c * eg).astype(BF16)
                kgt_scr[slot] = (k_c.T * f_scr[h, :, rows]).astype(BF16)
                ldc_scr[slot] = ld_scr[h, :, pl.ds(r0, LANES)]
                rhs = jnp.concatenate([v_ref[0, hh, rows, :],
                                       (k_c * eg).astype(BF16)], axis=1)
                items.append(dict(a=a_mat.astype(BF16), x=eye - a_mat * mask_scr[0].astype(F32),
                                  rhs=rhs, slot=slot))
        yield
        for lvl in range(1, N_LEVELS):
            for it in items:
                it["xb"] = it["x"].astype(BF16)
                it["mx"] = jnp.dot(it["a"] * mask_scr[lvl], it["xb"], preferred_element_type=F32)
            yield
            for it in items:
                it["x"] = it["x"] - jnp.dot(it["xb"], it["mx"].astype(BF16),
                                            preferred_element_type=F32)
            yield
        for it in items:
            uw = jnp.dot(it["x"].astype(BF16), it["rhs"], preferred_element_type=F32)
            u_scr[it["slot"]] = uw[:, :LANES]
            wq_scr[it["slot"], 0:CHUNK, :] = uw[:, LANES:].astype(BF16)
        yield

    def step_group(first_chunk, slot0, states, ties=()):
        for g in range(group):
            r0 = pl.multiple_of((first_chunk + g) * CHUNK, CHUNK)
            rows = pl.ds(r0, CHUNK)
            ws = [jnp.dot(wq_scr[slot0 + hh * group + g], states[hh].astype(BF16),
                          preferred_element_type=F32) for hh in range(HEADS_PER_STEP)]
            yield
            for hh in range(HEADS_PER_STEP):
                slot = slot0 + hh * group + g
                v_new = (u_scr[slot] - ws[hh][:CHUNK]).astype(BF16)
                o_c = ws[hh][CHUNK:] + jnp.dot(intra_scr[slot], v_new,
                                               preferred_element_type=F32)
                if ties:
                    o_c = _add_tile(o_c, ties.pop())
                o_scr[hh, rows, :] = o_c
                states[hh] = states[hh] * ldc_scr[slot] + jnp.dot(
                    kgt_scr[slot], v_new, preferred_element_type=F32)
            yield

    def interleave(*gens):
        live = list(gens)
        while live:
            for gen in list(live):
                try:
                    next(gen)
                except StopIteration:
                    live.remove(gen)

    assert nc == 2 * group and n_tiles % 2 == 0

    def out_rows(rows):
        tops = []
        for hh in range(HEADS_PER_STEP):
            o = o_scr[hh, rows, :]
            o = o * lax.rsqrt(jnp.mean(o * o, axis=-1, keepdims=True) + EPS) * nw_ref[...]
            res = o * _silu(z_ref[0, hh, rows, :].astype(F32))
            o_ref[hh, rows, :] = res.astype(o_ref.dtype)
            tops.append(res)
        return _tile_zero(functools.reduce(jnp.maximum, tops))

    def out_tiles(ties):
        for r in range(n_tiles):
            ties.append(out_rows(pl.ds(r * rt, rt)))
            for _ in range(2 * group // n_tiles):
                yield

    states = [s_scr[h] for h in heads_prev]
    interleave(prep_group(0, 0), step_group(group, set_size, states))
    for hh, h in enumerate(heads_prev):
        s_scr[h] = states[hh]

    out_ties = []
    states = [s_scr[h] for h in heads]
    interleave(out_tiles(out_ties), step_group(0, 0, states, out_ties),
               prep_group(group, set_size))
    for hh, h in enumerate(heads):
        s_scr[h] = states[hh]


def _delta(proj4, ba3, alog_rep, dtb_rep, norm_row, *, batch, seq, ts=2048, group=8, rt=256):
    t = proj4.shape[2]
    ns = seq // ts
    hps = HEADS_PER_STEP
    pairs = N_GROUPS // hps
    n_items = batch * ns * pairs
    cur = lambda s: jnp.minimum(s, n_items - 1)
    prv = lambda s: jnp.maximum(s - 1, 0)

    def slab(role, which):
        return pl.BlockSpec((1, hps, ts, LANES),
                            lambda s: (role, which(s) % pairs, which(s) // pairs, 0))

    rep = pl.BlockSpec((N_GROUPS, LANES), lambda s: (0, 0))
    n_slots = 2 * hps * group
    return pl.pallas_call(
        functools.partial(_delta_kernel, ts=ts, group=group, rt=rt, n_items=n_items,
                          blocks_per_seq=ns),
        out_shape=jax.ShapeDtypeStruct((N_GROUPS, t, LANES), BF16),
        grid=(n_items + 1,),
        in_specs=[
            slab(R_Q, cur), slab(R_K, cur), slab(R_V, cur), slab(R_Z, prv),
            pl.BlockSpec((1, ts, LANES), lambda s: (0, cur(s) // pairs, 0)),
            rep, rep,
            pl.BlockSpec((1, LANES), lambda s: (0, 0)),
        ],
        out_specs=pl.BlockSpec((hps, ts, LANES), lambda s: (prv(s) % pairs, prv(s) // pairs, 0)),
        scratch_shapes=[
            pltpu.VMEM((N_GROUPS, 1, ts), F32),
            pltpu.VMEM((N_GROUPS, 1, ts), F32),
            pltpu.VMEM((N_GROUPS, 1, ts), F32),
            pltpu.VMEM((N_GROUPS, 1, ts), F32),
            pltpu.VMEM((N_LEVELS, CHUNK, CHUNK), BF16),
            pltpu.VMEM((n_slots, CHUNK, LANES), F32),
            pltpu.VMEM((n_slots, 2 * CHUNK, LANES), BF16),
            pltpu.VMEM((n_slots, LANES, CHUNK), BF16),
            pltpu.VMEM((n_slots, CHUNK, CHUNK), BF16),
            pltpu.VMEM((n_slots, 1, LANES), F32),
            pltpu.VMEM((hps, ts, LANES), F32),
            pltpu.VMEM((N_GROUPS, LANES, LANES), F32),
        ],
        compiler_params=pltpu.CompilerParams(
            dimension_semantics=("arbitrary",),
            vmem_limit_bytes=VMEM_LIMIT),
        name="delta",
    )(proj4, proj4, proj4, proj4, ba3, alog_rep, dtb_rep, norm_row)


def _mlp_kernel(x_ref, val_ref, gate_ref, bv_ref, bg_ref, dw_ref, cb_ref, lg_ref, lb_ref,
                dn_ref, wout_ref, n2_ref, wup_ref, wdn_ref, fn_ref,
                o_ref, r_scr, h_scr, halo_scr, conv_scr, *, tm, rt, blocks_per_seq):
    i = pl.program_id(0)
    f = pl.program_id(1)
    n_blocks = pl.num_programs(0) - 1
    first_tap = CONF_HALO - (CONV_WIDTH - 1)

    def conf_part():
        seq_start = (i % blocks_per_seq) == 0
        h_scr[0:CONF_HALO, :] = jnp.where(seq_start, 0.0, halo_scr[f])
        a = val_ref[0, 0].astype(F32) + bv_ref[0]
        g = gate_ref[0, 0].astype(F32) + bg_ref[0]
        glu = a * _sigmoid(g)
        h_scr[CONF_HALO:CONF_HALO + tm, :] = glu
        zero_glu = _tile_zero(glu)
        halo_scr[f] = h_scr[tm:tm + CONF_HALO, :]
        slot = i % 2
        zeros = []
        for r in range(tm // rt):
            base = r * rt
            acc = dw_ref[0, 0:1, :] * h_scr[base + first_tap:base + first_tap + rt, :]
            for k in range(1, CONV_WIDTH):
                lo = base + first_tap + k
                acc = acc + dw_ref[0, k:k + 1, :] * h_scr[lo:lo + rt, :]
            y = acc + cb_ref[0]
            mu = jnp.mean(y, axis=-1, keepdims=True)
            cen = y - mu
            var = jnp.mean(cen * cen, axis=-1, keepdims=True)
            hn = cen * lax.rsqrt(var + EPS) * lg_ref[0] + lb_ref[0]
            out = _silu(hn)
            conv_scr[slot, f, base:base + rt, :] = out.astype(BF16)
            zeros.append(_tile_zero(out))
        return zero_glu, zeros

    def mlp_head():
        @pl.when(f == 0)
        def _():
            slot = (i + 1) % 2
            mix = jnp.concatenate([conv_scr[slot, c] for c in range(N_GROUPS)]
                                  + [dn_ref[c] for c in range(N_GROUPS)], axis=-1)
            h1 = x_ref[...] + jnp.dot(mix, wout_ref[...], preferred_element_type=F32)
            o_ref[...] = h1
            ms = jnp.mean(h1 * h1, axis=-1, keepdims=True)
            r_scr[...] = (h1 * lax.rsqrt(ms + EPS) * n2_ref[...]).astype(BF16)

    def mlp_body(zero_glu=None, zeros=None):
        m = jnp.dot(r_scr[...], wup_ref[...], preferred_element_type=F32)
        n_up = 0
        if zeros is not None:
            up_slab = 2 * LANES
            n_up = m.shape[1] // up_slab
            up_ties = [zero_glu] + zeros[:n_up - 1]
            m = jnp.concatenate(
                [jnp.concatenate([_add_tile(m[:, c * up_slab:c * up_slab + LANES], up_ties[c]),
                                  m[:, c * up_slab + LANES:(c + 1) * up_slab]], axis=1)
                 for c in range(n_up)], axis=1)
        act = jnp.square(jnp.maximum(m, 0.0)).astype(BF16)
        res = jnp.dot(act, wdn_ref[...], preferred_element_type=F32)
        if zeros is None:
            o_ref[...] += res
            return
        slab = o_ref.shape[1] // len(zeros)
        first = n_up - 1
        o_ref[:, 0:first * slab] += res[:, 0:first * slab]
        for n in range(first, len(zeros)):
            c0 = n * slab
            o_ref[:, c0 + LANES:c0 + slab] += res[:, c0 + LANES:c0 + slab]
            o_ref[:, c0:c0 + LANES] += _add_tile(res[:, c0:c0 + LANES], zeros[n])

    def mlp_tail():
        @pl.when(f == pl.num_programs(1) - 1)
        def _():
            h2 = o_ref[...]
            ms = jnp.mean(h2 * h2, axis=-1, keepdims=True)
            o_ref[...] = h2 * lax.rsqrt(ms + EPS) * fn_ref[...]

    @pl.when(i == 0)
    def _():
        @pl.when(f == 0)
        def _():
            halo_scr[...] = jnp.zeros_like(halo_scr)
        conf_part()

    @pl.when(jnp.logical_and(i > 0, i < n_blocks))
    def _():
        mlp_head()
        mlp_body(*conf_part())
        mlp_tail()

    @pl.when(i == n_blocks)
    def _():
        mlp_head()
        mlp_body()
        mlp_tail()


def _mlp(x2, proj4, b_glu, dw_w, dw_b, ln_g, ln_b, dn_o, w_out_b, n2, w_up_b, w_dn_b, fn,
         *, seq, tm=512, rt=64):
    t, d = x2.shape
    dff = w_up_b.shape[1]
    nb = t // tm
    nf = N_GROUPS
    tf = dff // nf
    prev = lambda i: jnp.maximum(i - 1, 0)
    nxt = lambda i: jnp.minimum(i, nb - 1)
    vec = lambda off: pl.BlockSpec((1, 1, LANES), lambda i, f: (off + f, 0, 0))
    return pl.pallas_call(
        functools.partial(_mlp_kernel, tm=tm, rt=rt, blocks_per_seq=seq // tm),
        out_shape=jax.ShapeDtypeStruct((t, d), F32),
        grid=(nb + 1, nf),
        in_specs=[
            pl.BlockSpec((tm, d), lambda i, f: (prev(i), 0)),
            pl.BlockSpec((1, 1, tm, LANES), lambda i, f: (R_VAL, f, nxt(i), 0)),
            pl.BlockSpec((1, 1, tm, LANES), lambda i, f: (R_GATE, f, nxt(i), 0)),
            vec(0), vec(N_GROUPS),
            pl.BlockSpec((1, CONV_WIDTH, LANES), lambda i, f: (f, 0, 0)),
            vec(0), vec(0), vec(0),
            pl.BlockSpec((N_GROUPS, tm, LANES), lambda i, f: (0, prev(i), 0)),
            pl.BlockSpec((d, d), lambda i, f: (0, 0), pipeline_mode=pl.Buffered(1)),
            pl.BlockSpec((1, d), lambda i, f: (0, 0)),
            pl.BlockSpec((d, tf), lambda i, f: (0, jnp.where(i == 0, 0, f))),
            pl.BlockSpec((tf, d), lambda i, f: (jnp.where(i == 0, 0, f), 0)),
            pl.BlockSpec((1, d), lambda i, f: (0, 0)),
        ],
        out_specs=pl.BlockSpec((tm, d), lambda i, f: (prev(i), 0)),
        scratch_shapes=[
            pltpu.VMEM((tm, d), BF16),
            pltpu.VMEM((CONF_HALO + tm, LANES), F32),
            pltpu.VMEM((N_GROUPS, CONF_HALO, LANES), F32),
            pltpu.VMEM((2, N_GROUPS, tm, LANES), BF16),
        ],
        compiler_params=pltpu.CompilerParams(
            dimension_semantics=("arbitrary", "arbitrary"),
            vmem_limit_bytes=VMEM_LIMIT),
        name="mlp",
    )(x2, proj4, proj4, b_glu, b_glu, dw_w, dw_b, ln_g, ln_b, dn_o,
      w_out_b, n2, w_up_b, w_dn_b, fn)


def _lane_rep(vals):
    return jnp.broadcast_to(vals.astype(F32)[:, None], (vals.shape[0], LANES))


def kernel(x, norm1_w, w_in, b_glu, conf_dw_w, conf_dw_b, conf_ln_g, conf_ln_b, dn_conv_w,
           dn_a_log, dn_dt_bias, dn_norm_w, w_out, norm2_w, w_mlp_up, w_mlp_down, final_norm_w):
    batch, seq, d = x.shape
    assert norm1_w.shape[0] == 1, "single-layer block"
    h = x.reshape(batch * seq, d)

    n_main = N_ROLES * N_GROUPS * LANES
    pair_w = HEADS_PER_STEP * LANES
    src = lambda role, hp: W_IN_ROLE_ORDER.index(role) * N_GROUPS * LANES + hp * pair_w
    w_pairs = jnp.concatenate(
        [w_in[0][:, src(role, hp):src(role, hp) + pair_w]
         for hp in range(N_GROUPS // HEADS_PER_STEP) for role in range(N_ROLES)],
        axis=1).astype(BF16)
    w_ba = jnp.pad(w_in[0][:, n_main:].astype(BF16),
                   ((0, 0), (0, LANES - (w_in.shape[2] - n_main))))
    conv_w4 = dn_conv_w[0].reshape(SHORT_CONV_WIDTH, 3, N_GROUPS, LANES)
    proj4, ba3 = _inproj(h, norm1_w[0].reshape(1, d), w_pairs, w_ba, conv_w4, seq=seq)

    dn_o = _delta(proj4, ba3, _lane_rep(dn_a_log[0]), _lane_rep(dn_dt_bias[0]),
                  dn_norm_w[0].reshape(1, LANES), batch=batch, seq=seq)

    grp = lambda v: v.reshape(-1, 1, LANES)
    dw_w = conf_dw_w[0].reshape(CONV_WIDTH, N_GROUPS, LANES).transpose(1, 0, 2)
    out = _mlp(h, proj4, grp(b_glu[0]), dw_w, grp(conf_dw_b[0]), grp(conf_ln_g[0]),
               grp(conf_ln_b[0]), dn_o, w_out[0].astype(BF16), norm2_w[0].reshape(1, d),
               w_mlp_up[0].astype(BF16), w_mlp_down[0].astype(BF16),
               final_norm_w.reshape(1, d), seq=seq)
    return out.reshape(batch, seq, d)
```

```python
import functools

import jax
import jax.numpy as jnp
from jax import lax
from jax.experimental import pallas as pl
from jax.experimental.pallas import tpu as pltpu

F32 = jnp.float32
BF16 = jnp.bfloat16

EPS = 1e-6
LANES = 128
SUBLANES = 8
CONV_WIDTH = 31
SHORT_CONV_WIDTH = 4
N_GROUPS = 8
CONF_HALO = 32
DN_HALO = SUBLANES
CHUNK = 128
N_LEVELS = CHUNK.bit_length() - 1
HEADS_PER_STEP = 2

CB_VAL, CB_GATE, CB_Q, CB_K, CB_V, CB_Z, CB_BA = 0, 8, 16, 24, 32, 40, 48
N_COLBLOCKS = 50

VMEM_LIMIT = 56 * 1024 * 1024


def _sigmoid(x):
    return 0.5 * jnp.tanh(0.5 * x) + 0.5


def _derived_zero(x):
    bits = lax.bitcast_convert_type(x, jnp.uint32)
    bits = lax.shift_right_logical(lax.shift_right_logical(bits, jnp.uint32(16)), jnp.uint32(16))
    return lax.bitcast_convert_type(bits, F32)


def _tile_zero(x):
    return _derived_zero(jnp.max(x.reshape(-1, SUBLANES, x.shape[-1]), axis=0))


def _add_tile(x, tile):
    n = tile.shape[0]
    return jnp.concatenate([x[0:n] + tile, x[n:]], axis=0)


def _softplus(x):
    return jnp.maximum(x, 0.0) + jnp.log1p(jnp.exp(-jnp.abs(x)))


def _inproj_kernel(x_ref, nw_ref, w_ref, o_ref, u_scr):
    @pl.when(pl.program_id(1) == 0)
    def _():
        x = x_ref[...]
        ms = jnp.mean(x * x, axis=-1, keepdims=True)
        u_scr[...] = (x * lax.rsqrt(ms + EPS) * nw_ref[...]).astype(BF16)

    res = jnp.dot(u_scr[...], w_ref[...], preferred_element_type=F32)
    for c in range(o_ref.shape[0]):
        o_ref[c] = res[:, c * LANES:(c + 1) * LANES]


def _inproj(x2, norm_w, w_in_p, *, tm=1024, tn=1280):
    t, d = x2.shape
    n = w_in_p.shape[1]
    return pl.pallas_call(
        _inproj_kernel,
        out_shape=jax.ShapeDtypeStruct((n // LANES, t, LANES), F32),
        grid=(t // tm, n // tn),
        in_specs=[
            pl.BlockSpec((tm, d), lambda i, j: (i, 0)),
            pl.BlockSpec((1, d), lambda i, j: (0, 0)),
            pl.BlockSpec((d, tn), lambda i, j: (0, j)),
        ],
        out_specs=pl.BlockSpec((tn // LANES, tm, LANES), lambda i, j: (j, i, 0)),
        scratch_shapes=[pltpu.VMEM((tm, d), BF16)],
        compiler_params=pltpu.CompilerParams(
            dimension_semantics=("arbitrary", "arbitrary"),
            vmem_limit_bytes=VMEM_LIMIT),
        name="inproj",
    )(x2, norm_w, w_in_p)


def _delta_kernel(q_ref, k_ref, v_ref, z_ref, ba_ref, cw_ref, alog_ref, dtb_ref, nw_ref,
                  o_ref,
                  x_scr, halo_scr, q_scr, k_scr, v_scr,
                  beta_scr, cs_scr, f_scr, ld_scr, mask_scr,
                  u_scr, wq_scr, kgt_scr, intra_scr, ldc_scr, o_scr, s_scr,
                  *, ts, group, rt, n_items, blocks_per_seq):
    step = pl.program_id(0)
    item = jnp.minimum(step, n_items - 1)
    prev = jnp.maximum(step - 1, 0)
    pairs = N_GROUPS // HEADS_PER_STEP
    hp = item % pairs
    i = (item // pairs) % blocks_per_seq
    heads = [HEADS_PER_STEP * hp + hh for hh in range(HEADS_PER_STEP)]
    heads_prev = [HEADS_PER_STEP * (prev % pairs) + hh for hh in range(HEADS_PER_STEP)]
    nc = ts // CHUNK

    r_io = lax.broadcasted_iota(jnp.int32, (CHUNK, CHUNK), 0)
    c_io = lax.broadcasted_iota(jnp.int32, (CHUNK, CHUNK), 1)
    causal = r_io >= c_io
    eye = jnp.where(r_io == c_io, 1.0, 0.0).astype(F32)

    @pl.when(hp == 0)
    def _():
        upper = jnp.where(r_io <= c_io, 1.0, 0.0).astype(F32)
        ones = jnp.ones((CHUNK, CHUNK), F32)
        for lvl in range(N_LEVELS):
            same = (r_io >> (lvl + 1)) == (c_io >> (lvl + 1))
            lower_left = jnp.where(((r_io >> lvl) & 1) == 1, 1.0, 0.0) * \
                jnp.where(((c_io >> lvl) & 1) == 0, 1.0, 0.0)
            mask_scr[lvl] = jnp.where(same, lower_left, 0.0).astype(BF16)
        ba_t = ba_ref[0].T
        beta = _sigmoid(ba_t[0:N_GROUPS, :])
        a_raw = ba_t[N_GROUPS:2 * N_GROUPS, :]
        neg_a = -jnp.exp(alog_ref[...])
        for c in range(nc):
            cols = slice(c * CHUNK, (c + 1) * CHUNK)
            g_c = neg_a * _softplus(a_raw[:, cols] + dtb_ref[...])
            cs = jnp.dot(g_c, upper, precision=lax.Precision.HIGHEST,
                         preferred_element_type=F32)
            tot = jnp.dot(g_c, ones, precision=lax.Precision.HIGHEST,
                          preferred_element_type=F32)
            f_c = jnp.exp(tot - cs) * beta[:, cols]
            ld_c = jnp.exp(tot)
            for hh in range(N_GROUPS):
                one = slice(hh, hh + 1)
                beta_scr[hh, :, cols] = beta[one, cols]
                cs_scr[hh, :, cols] = cs[one, :]
                f_scr[hh, :, cols] = f_c[one, :]
                ld_scr[hh, :, cols] = ld_c[one, :]

    @pl.when(i == 0)
    def _():
        for h in heads:
            halo_scr[h] = jnp.zeros((3, DN_HALO, LANES), F32)
            s_scr[h] = jnp.zeros((LANES, LANES), F32)

    set_size = HEADS_PER_STEP * group

    @pl.when(step == 0)
    def _():
        late = slice(set_size, 2 * set_size)
        u_scr[late] = jnp.zeros((set_size,) + u_scr.shape[1:], F32)
        wq_scr[late] = jnp.zeros((set_size,) + wq_scr.shape[1:], BF16)
        kgt_scr[late] = jnp.zeros((set_size,) + kgt_scr.shape[1:], BF16)
        intra_scr[late] = jnp.zeros((set_size,) + intra_scr.shape[1:], BF16)
        ldc_scr[late] = jnp.zeros((set_size,) + ldc_scr.shape[1:], F32)
        o_scr[...] = jnp.zeros_like(o_scr)

    refs = (q_ref, k_ref, v_ref)
    back = SHORT_CONV_WIDTH - 1
    for hh, h in enumerate(heads):
        for j, ref in enumerate(refs):
            x_scr[j, hh, 0:DN_HALO, :] = halo_scr[h, j]
            x_scr[j, hh, DN_HALO:DN_HALO + rt, :] = ref[hh, 0:rt, :]
            halo_scr[h, j] = ref[hh, ts - DN_HALO:ts, :]

    def conv_rows(rows, load):
        tops = []
        for hh, h in enumerate(heads):
            ys = []
            for j in range(3):
                hv = (0.5 * cw_ref[0, j, pl.ds(h, 1), :]) * load(j, hh, 0)
                for m in range(1, SHORT_CONV_WIDTH):
                    hv = hv + (0.5 * cw_ref[m, j, pl.ds(h, 1), :]) * load(j, hh, m)
                ys.append(hv + hv * jnp.tanh(hv))
            qn = ys[0] * (lax.rsqrt(jnp.sum(ys[0] * ys[0], axis=-1, keepdims=True) + EPS)
                          * (LANES ** -0.5))
            kn = ys[1] * lax.rsqrt(jnp.sum(ys[1] * ys[1], axis=-1, keepdims=True) + EPS)
            q_scr[hh, rows, :] = qn
            k_scr[hh, rows, :] = kn
            v_scr[hh, rows, :] = ys[2].astype(BF16)
            tops.append(jnp.maximum(jnp.maximum(qn, kn), ys[2]))
        return _tile_zero(functools.reduce(jnp.maximum, tops))

    conv_rows(pl.ds(0, rt),
              lambda j, hh, m: x_scr[j, hh, DN_HALO - back + m:DN_HALO - back + m + rt, :])

    def conv_tile(r, carry):
        base = pl.multiple_of(r * rt, rt)
        conv_rows(pl.ds(base, rt),
                  lambda j, hh, m: refs[j][hh, pl.ds(base - back + m, rt), :])
        return carry

    n_tiles = ts // rt
    lax.fori_loop(1, n_tiles // 2, conv_tile, 0)

    def conv_late(ties):
        for r in range(n_tiles // 2, n_tiles):
            base = r * rt
            ties.append(conv_rows(
                pl.ds(base, rt),
                lambda j, hh, m: refs[j][hh, base - back + m:base - back + m + rt, :]))
            yield
            yield
            yield

    def prep_group(first_chunk, slot0, ties=()):
        def take_ties():
            while ties:
                it = items[len(ties) % len(items)]
                it["x"] = _add_tile(it["x"], ties.pop())

        items = []
        for hh, h in enumerate(heads):
            for g in range(group):
                slot = slot0 + hh * group + g
                r0 = pl.multiple_of((first_chunk + g) * CHUNK, CHUNK)
                rows = pl.ds(r0, CHUNK)
                k_c = k_scr[hh, rows, :]
                qs_c = q_scr[hh, rows, :]
                cs_r = jnp.broadcast_to(cs_scr[h, :, rows], (CHUNK, CHUNK))
                beta_r = jnp.broadcast_to(beta_scr[h, :, rows], (CHUNK, CHUNK))
                cs_c = cs_r.T
                db = jnp.exp(jnp.where(causal, cs_c - cs_r, -jnp.inf)) * beta_r
                kq = lax.dot_general(
                    jnp.concatenate([k_c, qs_c], axis=0).astype(BF16), k_c.astype(BF16),
                    (((1,), (1,)), ((), ())), preferred_element_type=F32)
                a_mat = kq[:CHUNK] * db
                intra_scr[slot] = (kq[CHUNK:] * db).astype(BF16)
                eg = jnp.exp(cs_c)
                wq_scr[slot, CHUNK:2 * CHUNK, :] = (qs_c * eg).astype(BF16)
                kgt_scr[slot] = (k_c.T * f_scr[h, :, rows]).astype(BF16)
                ldc_scr[slot] = ld_scr[h, :, pl.ds(r0, LANES)]
                rhs = jnp.concatenate([v_scr[hh, rows, :], (k_c * eg).astype(BF16)], axis=1)
                items.append(dict(a=a_mat.astype(BF16), x=eye - a_mat * mask_scr[0].astype(F32),
                                  rhs=rhs, slot=slot))
        yield
        for lvl in range(1, N_LEVELS):
            take_ties()
            for it in items:
                it["xb"] = it["x"].astype(BF16)
                it["mx"] = jnp.dot(it["a"] * mask_scr[lvl], it["xb"], preferred_element_type=F32)
            yield
            for it in items:
                it["x"] = it["x"] - jnp.dot(it["xb"], it["mx"].astype(BF16),
                                            preferred_element_type=F32)
            yield
        take_ties()
        for it in items:
            uw = jnp.dot(it["x"].astype(BF16), it["rhs"], preferred_element_type=F32)
            u_scr[it["slot"]] = uw[:, :LANES]
            wq_scr[it["slot"], 0:CHUNK, :] = uw[:, LANES:].astype(BF16)
        yield

    def step_group(first_chunk, slot0, states, ties=()):
        for g in range(group):
            r0 = pl.multiple_of((first_chunk + g) * CHUNK, CHUNK)
            rows = pl.ds(r0, CHUNK)
            ws = [jnp.dot(wq_scr[slot0 + hh * group + g], states[hh].astype(BF16),
                          preferred_element_type=F32) for hh in range(HEADS_PER_STEP)]
            yield
            for hh in range(HEADS_PER_STEP):
                slot = slot0 + hh * group + g
                v_new = (u_scr[slot] - ws[hh][:CHUNK]).astype(BF16)
                o_c = ws[hh][CHUNK:] + jnp.dot(intra_scr[slot], v_new,
                                               preferred_element_type=F32)
                if ties:
                    o_c = _add_tile(o_c, ties.pop())
                o_scr[hh, rows, :] = o_c
                states[hh] = states[hh] * ldc_scr[slot] + jnp.dot(
                    kgt_scr[slot], v_new, preferred_element_type=F32)
            yield

    def interleave(*gens):
        live = list(gens)
        while live:
            for gen in list(live):
                try:
                    next(gen)
                except StopIteration:
                    live.remove(gen)

    assert nc == 2 * group and n_tiles % 2 == 0

    def out_rows(rows):
        tops = []
        for hh in range(HEADS_PER_STEP):
            o = o_scr[hh, rows, :]
            o = o * lax.rsqrt(jnp.mean(o * o, axis=-1, keepdims=True) + EPS) * nw_ref[...]
            hz = 0.5 * z_ref[hh, rows, :]
            res = o * (hz + hz * jnp.tanh(hz))
            o_ref[hh, rows, :] = res.astype(o_ref.dtype)
            tops.append(res)
        return _tile_zero(functools.reduce(jnp.maximum, tops))

    def out_tiles(ties):
        for r in range(n_tiles):
            ties.append(out_rows(pl.ds(r * rt, rt)))
            for _ in range(2 * group // n_tiles):
                yield

    conv_ties = []
    states = [s_scr[h] for h in heads_prev]
    interleave(conv_late(conv_ties), prep_group(0, 0, conv_ties),
               step_group(group, set_size, states))
    for hh, h in enumerate(heads_prev):
        s_scr[h] = states[hh]

    out_ties = []
    states = [s_scr[h] for h in heads]
    interleave(out_tiles(out_ties), step_group(0, 0, states, out_ties),
               prep_group(group, set_size))
    for hh, h in enumerate(heads):
        s_scr[h] = states[hh]


def _delta(proj3, conv_w4, alog_rep, dtb_rep, norm_row, *, batch, seq, ts=2048, group=8, rt=256):
    t = proj3.shape[1]
    ns = seq // ts
    hps = HEADS_PER_STEP
    pairs = N_GROUPS // hps
    n_items = batch * ns * pairs

    def cur(off):
        def index_map(s):
            it = jnp.minimum(s, n_items - 1)
            return (off // hps + it % pairs, it // pairs, 0)
        return index_map

    def prv(off):
        def index_map(s):
            it = jnp.maximum(s - 1, 0)
            return (off // hps + it % pairs, it // pairs, 0)
        return index_map

    blk = lambda index_map: pl.BlockSpec((hps, ts, LANES), index_map)
    rep = pl.BlockSpec((N_GROUPS, LANES), lambda s: (0, 0))
    n_slots = 2 * hps * group
    return pl.pallas_call(
        functools.partial(_delta_kernel, ts=ts, group=group, rt=rt, n_items=n_items,
                          blocks_per_seq=ns),
        out_shape=jax.ShapeDtypeStruct((N_GROUPS, t, LANES), BF16),
        grid=(n_items + 1,),
        in_specs=[
            blk(cur(CB_Q)), blk(cur(CB_K)), blk(cur(CB_V)), blk(prv(CB_Z)),
            pl.BlockSpec((1, ts, LANES),
                         lambda s: (CB_BA, jnp.minimum(s, n_items - 1) // pairs, 0)),
            pl.BlockSpec((SHORT_CONV_WIDTH, 3, N_GROUPS, LANES), lambda s: (0, 0, 0, 0)),
            rep, rep,
            pl.BlockSpec((1, LANES), lambda s: (0, 0)),
        ],
        out_specs=blk(prv(0)),
        scratch_shapes=[
            pltpu.VMEM((3, hps, DN_HALO + rt, LANES), F32),
            pltpu.VMEM((N_GROUPS, 3, DN_HALO, LANES), F32),
            pltpu.VMEM((hps, ts, LANES), F32),
            pltpu.VMEM((hps, ts, LANES), F32),
            pltpu.VMEM((hps, ts, LANES), BF16),
            pltpu.VMEM((N_GROUPS, 1, ts), F32),
            pltpu.VMEM((N_GROUPS, 1, ts), F32),
            pltpu.VMEM((N_GROUPS, 1, ts), F32),
            pltpu.VMEM((N_GROUPS, 1, ts), F32),
            pltpu.VMEM((N_LEVELS, CHUNK, CHUNK), BF16),
            pltpu.VMEM((n_slots, CHUNK, LANES), F32),
            pltpu.VMEM((n_slots, 2 * CHUNK, LANES), BF16),
            pltpu.VMEM((n_slots, LANES, CHUNK), BF16),
            pltpu.VMEM((n_slots, CHUNK, CHUNK), BF16),
            pltpu.VMEM((n_slots, 1, LANES), F32),
            pltpu.VMEM((hps, ts, LANES), F32),
            pltpu.VMEM((N_GROUPS, LANES, LANES), F32),
        ],
        compiler_params=pltpu.CompilerParams(
            dimension_semantics=("arbitrary",),
            vmem_limit_bytes=VMEM_LIMIT),
        name="delta",
    )(proj3, proj3, proj3, proj3, proj3, conv_w4, alog_rep, dtb_rep, norm_row)


def _mlp_kernel(x_ref, val_ref, gate_ref, bv_ref, bg_ref, dw_ref, cb_ref, lg_ref, lb_ref,
                dn_ref, wout_ref, n2_ref, wup_ref, wdn_ref, fn_ref,
                o_ref, r_scr, h_scr, halo_scr, conv_scr, *, tm, rt, blocks_per_seq):
    i = pl.program_id(0)
    f = pl.program_id(1)
    n_blocks = pl.num_programs(0) - 1
    first_tap = CONF_HALO - (CONV_WIDTH - 1)

    def conf_part():
        seq_start = (i % blocks_per_seq) == 0
        h_scr[0:CONF_HALO, :] = jnp.where(seq_start, 0.0, halo_scr[f])
        a = val_ref[0] + bv_ref[0]
        g = gate_ref[0] + bg_ref[0]
        glu = a * _sigmoid(g)
        h_scr[CONF_HALO:CONF_HALO + tm, :] = glu
        zero_glu = _tile_zero(glu)
        halo_scr[f] = h_scr[tm:tm + CONF_HALO, :]
        slot = i % 2
        zeros = []
        for r in range(tm // rt):
            base = r * rt
            acc = dw_ref[0, 0:1, :] * h_scr[base + first_tap:base + first_tap + rt, :]
            for k in range(1, CONV_WIDTH):
                lo = base + first_tap + k
                acc = acc + dw_ref[0, k:k + 1, :] * h_scr[lo:lo + rt, :]
            y = acc + cb_ref[0]
            mu = jnp.mean(y, axis=-1, keepdims=True)
            cen = y - mu
            var = jnp.mean(cen * cen, axis=-1, keepdims=True)
            hn = cen * lax.rsqrt(var + EPS) * (0.5 * lg_ref[0]) + 0.5 * lb_ref[0]
            out = hn + hn * jnp.tanh(hn)
            conv_scr[slot, f, base:base + rt, :] = out.astype(BF16)
            zeros.append(_tile_zero(out))
        return zero_glu, zeros

    def mlp_head():
        @pl.when(f == 0)
        def _():
            slot = (i + 1) % 2
            mix = jnp.concatenate([conv_scr[slot, c] for c in range(N_GROUPS)]
                                  + [dn_ref[c] for c in range(N_GROUPS)], axis=-1)
            h1 = x_ref[...] + jnp.dot(mix, wout_ref[...], preferred_element_type=F32)
            o_ref[...] = h1
            ms = jnp.mean(h1 * h1, axis=-1, keepdims=True)
            r_scr[...] = (h1 * lax.rsqrt(ms + EPS) * n2_ref[...]).astype(BF16)

    def mlp_body(zero_glu=None, zeros=None):
        m = jnp.dot(r_scr[...], wup_ref[...], preferred_element_type=F32)
        n_up = 0
        if zeros is not None:
            up_slab = 2 * LANES
            n_up = m.shape[1] // up_slab
            up_ties = [zero_glu] + zeros[:n_up - 1]
            m = jnp.concatenate(
                [jnp.concatenate([_add_tile(m[:, c * up_slab:c * up_slab + LANES], up_ties[c]),
                                  m[:, c * up_slab + LANES:(c + 1) * up_slab]], axis=1)
                 for c in range(n_up)], axis=1)
        act = jnp.square(jnp.maximum(m, 0.0)).astype(BF16)
        res = jnp.dot(act, wdn_ref[...], preferred_element_type=F32)
        if zeros is None:
            o_ref[...] += res
            return
        slab = o_ref.shape[1] // len(zeros)
        first = n_up - 1
        o_ref[:, 0:first * slab] += res[:, 0:first * slab]
        for n in range(first, len(zeros)):
            c0 = n * slab
            o_ref[:, c0 + LANES:c0 + slab] += res[:, c0 + LANES:c0 + slab]
            o_ref[:, c0:c0 + LANES] += _add_tile(res[:, c0:c0 + LANES], zeros[n])

    def mlp_tail():
        @pl.when(f == pl.num_programs(1) - 1)
        def _():
            h2 = o_ref[...]
            ms = jnp.mean(h2 * h2, axis=-1, keepdims=True)
            o_ref[...] = h2 * lax.rsqrt(ms + EPS) * fn_ref[...]

    @pl.when(i == 0)
    def _():
        @pl.when(f == 0)
        def _():
            halo_scr[...] = jnp.zeros_like(halo_scr)
        conf_part()

    @pl.when(jnp.logical_and(i > 0, i < n_blocks))
    def _():
        mlp_head()
        mlp_body(*conf_part())
        mlp_tail()

    @pl.when(i == n_blocks)
    def _():
        mlp_head()
        mlp_body()
        mlp_tail()


def _mlp(x2, proj3, b_glu, dw_w, dw_b, ln_g, ln_b, dn_o, w_out_b, n2, w_up_b, w_dn_b, fn,
         *, seq, tm=512, rt=64):
    t, d = x2.shape
    dff = w_up_b.shape[1]
    nb = t // tm
    nf = N_GROUPS
    tf = dff // nf
    prev = lambda i: jnp.maximum(i - 1, 0)
    nxt = lambda i: jnp.minimum(i, nb - 1)
    vec = lambda off: pl.BlockSpec((1, 1, LANES), lambda i, f: (off + f, 0, 0))
    return pl.pallas_call(
        functools.partial(_mlp_kernel, tm=tm, rt=rt, blocks_per_seq=seq // tm),
        out_shape=jax.ShapeDtypeStruct((t, d), F32),
        grid=(nb + 1, nf),
        in_specs=[
            pl.BlockSpec((tm, d), lambda i, f: (prev(i), 0)),
            pl.BlockSpec((1, tm, LANES), lambda i, f: (CB_VAL + f, nxt(i), 0)),
            pl.BlockSpec((1, tm, LANES), lambda i, f: (CB_GATE + f, nxt(i), 0)),
            vec(0), vec(N_GROUPS),
            pl.BlockSpec((1, CONV_WIDTH, LANES), lambda i, f: (f, 0, 0)),
            vec(0), vec(0), vec(0),
            pl.BlockSpec((N_GROUPS, tm, LANES), lambda i, f: (0, prev(i), 0)),
            pl.BlockSpec((d, d), lambda i, f: (0, 0), pipeline_mode=pl.Buffered(1)),
            pl.BlockSpec((1, d), lambda i, f: (0, 0)),
            pl.BlockSpec((d, tf), lambda i, f: (0, jnp.where(i == 0, 0, f))),
            pl.BlockSpec((tf, d), lambda i, f: (jnp.where(i == 0, 0, f), 0)),
            pl.BlockSpec((1, d), lambda i, f: (0, 0)),
        ],
        out_specs=pl.BlockSpec((tm, d), lambda i, f: (prev(i), 0)),
        scratch_shapes=[
            pltpu.VMEM((tm, d), BF16),
            pltpu.VMEM((CONF_HALO + tm, LANES), F32),
            pltpu.VMEM((N_GROUPS, CONF_HALO, LANES), F32),
            pltpu.VMEM((2, N_GROUPS, tm, LANES), BF16),
        ],
        compiler_params=pltpu.CompilerParams(
            dimension_semantics=("arbitrary", "arbitrary"),
            vmem_limit_bytes=VMEM_LIMIT),
        name="mlp",
    )(x2, proj3, proj3, b_glu, b_glu, dw_w, dw_b, ln_g, ln_b, dn_o,
      w_out_b, n2, w_up_b, w_dn_b, fn)


def _lane_rep(vals):
    return jnp.broadcast_to(vals.astype(F32)[:, None], (vals.shape[0], LANES))


def kernel(x, norm1_w, w_in, b_glu, conf_dw_w, conf_dw_b, conf_ln_g, conf_ln_b, dn_conv_w,
           dn_a_log, dn_dt_bias, dn_norm_w, w_out, norm2_w, w_mlp_up, w_mlp_down, final_norm_w):
    batch, seq, d = x.shape
    assert norm1_w.shape[0] == 1, "single-layer block"
    h = x.reshape(batch * seq, d)

    d_in = w_in.shape[2]
    w_in_p = jnp.pad(w_in[0].astype(BF16), ((0, 0), (0, N_COLBLOCKS * LANES - d_in)))
    proj3 = _inproj(h, norm1_w[0].reshape(1, d), w_in_p)

    conv_w4 = dn_conv_w[0].reshape(SHORT_CONV_WIDTH, 3, N_GROUPS, LANES)
    dn_o = _delta(proj3, conv_w4, _lane_rep(dn_a_log[0]), _lane_rep(dn_dt_bias[0]),
                  dn_norm_w[0].reshape(1, LANES), batch=batch, seq=seq)

    grp = lambda v: v.reshape(-1, 1, LANES)
    dw_w = conf_dw_w[0].reshape(CONV_WIDTH, N_GROUPS, LANES).transpose(1, 0, 2)
    out = _mlp(h, proj3, grp(b_glu[0]), dw_w, grp(conf_dw_b[0]), grp(conf_ln_g[0]),
               grp(conf_ln_b[0]), dn_o, w_out[0].astype(BF16), norm2_w[0].reshape(1, d),
               w_mlp_up[0].astype(BF16), w_mlp_down[0].astype(BF16),
               final_norm_w.reshape(1, d), seq=seq)
    return out.reshape(batch, seq, d)
```

```python
import functools

import jax
import jax.numpy as jnp
from jax import lax
from jax.experimental import pallas as pl
from jax.experimental.pallas import tpu as pltpu

F32 = jnp.float32
BF16 = jnp.bfloat16

EPS = 1e-6
LANES = 128
SUBLANES = 8
CONV_WIDTH = 31
SHORT_CONV_WIDTH = 4
N_GROUPS = 8
CONF_HALO = 32
DN_HALO = SUBLANES
CHUNK = 128
N_LEVELS = CHUNK.bit_length() - 1
HEADS_PER_STEP = 2

CB_VAL, CB_GATE, CB_Q, CB_K, CB_V, CB_Z, CB_BA = 0, 8, 16, 24, 32, 40, 48
N_COLBLOCKS = 50

VMEM_LIMIT = 56 * 1024 * 1024


def _sigmoid(x):
    return 0.5 * jnp.tanh(0.5 * x) + 0.5


def _derived_zero(x):
    bits = lax.bitcast_convert_type(x, jnp.uint32)
    bits = lax.shift_right_logical(lax.shift_right_logical(bits, jnp.uint32(16)), jnp.uint32(16))
    return lax.bitcast_convert_type(bits, F32)


def _tile_zero(x):
    return _derived_zero(jnp.max(x.reshape(-1, SUBLANES, x.shape[-1]), axis=0))


def _add_tile(x, tile):
    n = tile.shape[0]
    return jnp.concatenate([x[0:n] + tile, x[n:]], axis=0)


def _softplus(x):
    return jnp.maximum(x, 0.0) + jnp.log1p(jnp.exp(-jnp.abs(x)))


def _inproj_kernel(x_ref, nw_ref, w_ref, o_ref, u_scr):
    @pl.when(pl.program_id(1) == 0)
    def _():
        x = x_ref[...]
        ms = jnp.mean(x * x, axis=-1, keepdims=True)
        u_scr[...] = (x * lax.rsqrt(ms + EPS) * nw_ref[...]).astype(BF16)

    res = jnp.dot(u_scr[...], w_ref[...], preferred_element_type=F32)
    for c in range(o_ref.shape[0]):
        o_ref[c] = res[:, c * LANES:(c + 1) * LANES]


def _inproj(x2, norm_w, w_in_p, *, tm=1024, tn=1280):
    t, d = x2.shape
    n = w_in_p.shape[1]
    return pl.pallas_call(
        _inproj_kernel,
        out_shape=jax.ShapeDtypeStruct((n // LANES, t, LANES), F32),
        grid=(t // tm, n // tn),
        in_specs=[
            pl.BlockSpec((tm, d), lambda i, j: (i, 0)),
            pl.BlockSpec((1, d), lambda i, j: (0, 0)),
            pl.BlockSpec((d, tn), lambda i, j: (0, j)),
        ],
        out_specs=pl.BlockSpec((tn // LANES, tm, LANES), lambda i, j: (j, i, 0)),
        scratch_shapes=[pltpu.VMEM((tm, d), BF16)],
        compiler_params=pltpu.CompilerParams(
            dimension_semantics=("arbitrary", "arbitrary"),
            vmem_limit_bytes=VMEM_LIMIT),
        name="inproj",
    )(x2, norm_w, w_in_p)


def _delta_kernel(q_ref, k_ref, v_ref, z_ref, ba_ref, cw_ref, alog_ref, dtb_ref, nw_ref,
                  wout_ref, wup_ref, wdn_ref,
                  o_ref, wout_o, wup_o, wdn_o,
                  x_scr, halo_scr, q_scr, k_scr, v_scr,
                  beta_scr, cs_scr, f_scr, ld_scr, mask_scr,
                  u_scr, wq_scr, kgt_scr, intra_scr, ldc_scr, o_scr, s_scr,
                  *, ts, group, rt, n_items, blocks_per_seq):
    step = pl.program_id(0)
    item = jnp.minimum(step, n_items - 1)
    prev = jnp.maximum(step - 1, 0)
    pairs = N_GROUPS // HEADS_PER_STEP
    hp = item % pairs
    i = (item // pairs) % blocks_per_seq
    heads = [HEADS_PER_STEP * hp + hh for hh in range(HEADS_PER_STEP)]
    heads_prev = [HEADS_PER_STEP * (prev % pairs) + hh for hh in range(HEADS_PER_STEP)]
    nc = ts // CHUNK

    wout_o[...] = wout_ref[...].astype(BF16)
    wup_o[...] = wup_ref[...].astype(BF16)
    wdn_o[...] = wdn_ref[...].astype(BF16)

    r_io = lax.broadcasted_iota(jnp.int32, (CHUNK, CHUNK), 0)
    c_io = lax.broadcasted_iota(jnp.int32, (CHUNK, CHUNK), 1)
    causal = r_io >= c_io
    eye = jnp.where(r_io == c_io, 1.0, 0.0).astype(F32)

    @pl.when(hp == 0)
    def _():
        upper = jnp.where(r_io <= c_io, 1.0, 0.0).astype(F32)
        ones = jnp.ones((CHUNK, CHUNK), F32)
        for lvl in range(N_LEVELS):
            same = (r_io >> (lvl + 1)) == (c_io >> (lvl + 1))
            lower_left = jnp.where(((r_io >> lvl) & 1) == 1, 1.0, 0.0) * \
                jnp.where(((c_io >> lvl) & 1) == 0, 1.0, 0.0)
            mask_scr[lvl] = jnp.where(same, lower_left, 0.0).astype(BF16)
        ba_t = ba_ref[0].T
        beta = _sigmoid(ba_t[0:N_GROUPS, :])
        a_raw = ba_t[N_GROUPS:2 * N_GROUPS, :]
        neg_a = -jnp.exp(alog_ref[...])
        for c in range(nc):
            cols = slice(c * CHUNK, (c + 1) * CHUNK)
            g_c = neg_a * _softplus(a_raw[:, cols] + dtb_ref[...])
            cs = jnp.dot(g_c, upper, precision=lax.Precision.HIGHEST,
                         preferred_element_type=F32)
            tot = jnp.dot(g_c, ones, precision=lax.Precision.HIGHEST,
                          preferred_element_type=F32)
            f_c = jnp.exp(tot - cs) * beta[:, cols]
            ld_c = jnp.exp(tot)
            for hh in range(N_GROUPS):
                one = slice(hh, hh + 1)
                beta_scr[hh, :, cols] = beta[one, cols]
                cs_scr[hh, :, cols] = cs[one, :]
                f_scr[hh, :, cols] = f_c[one, :]
                ld_scr[hh, :, cols] = ld_c[one, :]

    @pl.when(i == 0)
    def _():
        for h in heads:
            halo_scr[h] = jnp.zeros((3, DN_HALO, LANES), F32)
            s_scr[h] = jnp.zeros((LANES, LANES), F32)

    set_size = HEADS_PER_STEP * group

    @pl.when(step == 0)
    def _():
        late = slice(set_size, 2 * set_size)
        u_scr[late] = jnp.zeros((set_size,) + u_scr.shape[1:], F32)
        wq_scr[late] = jnp.zeros((set_size,) + wq_scr.shape[1:], BF16)
        kgt_scr[late] = jnp.zeros((set_size,) + kgt_scr.shape[1:], BF16)
        intra_scr[late] = jnp.zeros((set_size,) + intra_scr.shape[1:], BF16)
        ldc_scr[late] = jnp.zeros((set_size,) + ldc_scr.shape[1:], F32)
        o_scr[...] = jnp.zeros_like(o_scr)

    refs = (q_ref, k_ref, v_ref)
    back = SHORT_CONV_WIDTH - 1
    for hh, h in enumerate(heads):
        for j, ref in enumerate(refs):
            x_scr[j, hh, 0:DN_HALO, :] = halo_scr[h, j]
            x_scr[j, hh, DN_HALO:DN_HALO + rt, :] = ref[hh, 0:rt, :]
            halo_scr[h, j] = ref[hh, ts - DN_HALO:ts, :]

    def conv_rows(rows, load):
        tops = []
        for hh, h in enumerate(heads):
            ys = []
            for j in range(3):
                hv = (0.5 * cw_ref[0, j, pl.ds(h, 1), :]) * load(j, hh, 0)
                for m in range(1, SHORT_CONV_WIDTH):
                    hv = hv + (0.5 * cw_ref[m, j, pl.ds(h, 1), :]) * load(j, hh, m)
                ys.append(hv + hv * jnp.tanh(hv))
            qn = ys[0] * (lax.rsqrt(jnp.sum(ys[0] * ys[0], axis=-1, keepdims=True) + EPS)
                          * (LANES ** -0.5))
            kn = ys[1] * lax.rsqrt(jnp.sum(ys[1] * ys[1], axis=-1, keepdims=True) + EPS)
            q_scr[hh, rows, :] = qn
            k_scr[hh, rows, :] = kn
            v_scr[hh, rows, :] = ys[2].astype(BF16)
            tops.append(jnp.maximum(jnp.maximum(qn, kn), ys[2]))
        return _tile_zero(functools.reduce(jnp.maximum, tops))

    conv_rows(pl.ds(0, rt),
              lambda j, hh, m: x_scr[j, hh, DN_HALO - back + m:DN_HALO - back + m + rt, :])

    def conv_tile(r, carry):
        base = pl.multiple_of(r * rt, rt)
        conv_rows(pl.ds(base, rt),
                  lambda j, hh, m: refs[j][hh, pl.ds(base - back + m, rt), :])
        return carry

    n_tiles = ts // rt
    lax.fori_loop(1, n_tiles // 2, conv_tile, 0)

    def conv_late(ties):
        for r in range(n_tiles // 2, n_tiles):
            base = r * rt
            ties.append(conv_rows(
                pl.ds(base, rt),
                lambda j, hh, m: refs[j][hh, base - back + m:base - back + m + rt, :]))
            yield
            yield
            yield

    def prep_group(first_chunk, slot0, ties=()):
        def take_ties():
            while ties:
                it = items[len(ties) % len(items)]
                it["x"] = _add_tile(it["x"], ties.pop())

        items = []
        for hh, h in enumerate(heads):
            for g in range(group):
                slot = slot0 + hh * group + g
                r0 = pl.multiple_of((first_chunk + g) * CHUNK, CHUNK)
                rows = pl.ds(r0, CHUNK)
                k_c = k_scr[hh, rows, :]
                qs_c = q_scr[hh, rows, :]
                cs_r = jnp.broadcast_to(cs_scr[h, :, rows], (CHUNK, CHUNK))
                beta_r = jnp.broadcast_to(beta_scr[h, :, rows], (CHUNK, CHUNK))
                cs_c = cs_r.T
                db = jnp.exp(jnp.where(causal, cs_c - cs_r, -jnp.inf)) * beta_r
                kq = lax.dot_general(
                    jnp.concatenate([k_c, qs_c], axis=0).astype(BF16), k_c.astype(BF16),
                    (((1,), (1,)), ((), ())), preferred_element_type=F32)
                a_mat = kq[:CHUNK] * db
                intra_scr[slot] = (kq[CHUNK:] * db).astype(BF16)
                eg = jnp.exp(cs_c)
                wq_scr[slot, CHUNK:2 * CHUNK, :] = (qs_c * eg).astype(BF16)
                kgt_scr[slot] = (k_c.T * f_scr[h, :, rows]).astype(BF16)
                ldc_scr[slot] = ld_scr[h, :, pl.ds(r0, LANES)]
                rhs = jnp.concatenate([v_scr[hh, rows, :], (k_c * eg).astype(BF16)], axis=1)
                items.append(dict(a=a_mat.astype(BF16), x=eye - a_mat * mask_scr[0].astype(F32),
                                  rhs=rhs, slot=slot))
        yield
        for lvl in range(1, N_LEVELS):
            take_ties()
            for it in items:
                it["xb"] = it["x"].astype(BF16)
                it["mx"] = jnp.dot(it["a"] * mask_scr[lvl], it["xb"], preferred_element_type=F32)
            yield
            for it in items:
                it["x"] = it["x"] - jnp.dot(it["xb"], it["mx"].astype(BF16),
                                            preferred_element_type=F32)
            yield
        take_ties()
        for it in items:
            uw = jnp.dot(it["x"].astype(BF16), it["rhs"], preferred_element_type=F32)
            u_scr[it["slot"]] = uw[:, :LANES]
            wq_scr[it["slot"], 0:CHUNK, :] = uw[:, LANES:].astype(BF16)
        yield

    def step_group(first_chunk, slot0, states, ties=()):
        for g in range(group):
            r0 = pl.multiple_of((first_chunk + g) * CHUNK, CHUNK)
            rows = pl.ds(r0, CHUNK)
            ws = [jnp.dot(wq_scr[slot0 + hh * group + g], states[hh].astype(BF16),
                          preferred_element_type=F32) for hh in range(HEADS_PER_STEP)]
            yield
            for hh in range(HEADS_PER_STEP):
                slot = slot0 + hh * group + g
                v_new = (u_scr[slot] - ws[hh][:CHUNK]).astype(BF16)
                o_c = ws[hh][CHUNK:] + jnp.dot(intra_scr[slot], v_new,
                                               preferred_element_type=F32)
                if ties:
                    o_c = _add_tile(o_c, ties.pop())
                o_scr[hh, rows, :] = o_c
                states[hh] = states[hh] * ldc_scr[slot] + jnp.dot(
                    kgt_scr[slot], v_new, preferred_element_type=F32)
            yield

    def interleave(*gens):
        live = list(gens)
        while live:
            for gen in list(live):
                try:
                    next(gen)
                except StopIteration:
                    live.remove(gen)

    assert nc == 2 * group and n_tiles % 2 == 0

    def out_rows(rows):
        tops = []
        for hh in range(HEADS_PER_STEP):
            o = o_scr[hh, rows, :]
            o = o * lax.rsqrt(jnp.mean(o * o, axis=-1, keepdims=True) + EPS) * nw_ref[...]
            hz = 0.5 * z_ref[hh, rows, :]
            res = o * (hz + hz * jnp.tanh(hz))
            o_ref[hh, rows, :] = res.astype(o_ref.dtype)
            tops.append(res)
        return _tile_zero(functools.reduce(jnp.maximum, tops))

    def out_tiles(ties):
        for r in range(n_tiles):
            ties.append(out_rows(pl.ds(r * rt, rt)))
            for _ in range(2 * group // n_tiles):
                yield

    conv_ties = []
    states = [s_scr[h] for h in heads_prev]
    interleave(conv_late(conv_ties), prep_group(0, 0, conv_ties),
               step_group(group, set_size, states))
    for hh, h in enumerate(heads_prev):
        s_scr[h] = states[hh]

    out_ties = []
    states = [s_scr[h] for h in heads]
    interleave(out_tiles(out_ties), step_group(0, 0, states, out_ties),
               prep_group(group, set_size))
    for hh, h in enumerate(heads):
        s_scr[h] = states[hh]


def _delta(proj3, conv_w4, alog_rep, dtb_rep, norm_row, weights, *, batch, seq,
           ts=2048, group=8, rt=256):
    t = proj3.shape[1]
    ns = seq // ts
    hps = HEADS_PER_STEP
    pairs = N_GROUPS // hps
    n_items = batch * ns * pairs
    for w in weights:
        assert w.shape[0] % n_items == 0 and (w.shape[0] // n_items) % 16 == 0, w.shape
    w_spec = lambda w: pl.BlockSpec((w.shape[0] // n_items, w.shape[1]),
                                    lambda s: (jnp.minimum(s, n_items - 1), 0))

    def cur(off):
        def index_map(s):
            it = jnp.minimum(s, n_items - 1)
            return (off // hps + it % pairs, it // pairs, 0)
        return index_map

    def prv(off):
        def index_map(s):
            it = jnp.maximum(s - 1, 0)
            return (off // hps + it % pairs, it // pairs, 0)
        return index_map

    blk = lambda index_map: pl.BlockSpec((hps, ts, LANES), index_map)
    rep = pl.BlockSpec((N_GROUPS, LANES), lambda s: (0, 0))
    n_slots = 2 * hps * group
    return pl.pallas_call(
        functools.partial(_delta_kernel, ts=ts, group=group, rt=rt, n_items=n_items,
                          blocks_per_seq=ns),
        out_shape=(jax.ShapeDtypeStruct((N_GROUPS, t, LANES), BF16),)
        + tuple(jax.ShapeDtypeStruct(w.shape, BF16) for w in weights),
        grid=(n_items + 1,),
        in_specs=[
            blk(cur(CB_Q)), blk(cur(CB_K)), blk(cur(CB_V)), blk(prv(CB_Z)),
            pl.BlockSpec((1, ts, LANES),
                         lambda s: (CB_BA, jnp.minimum(s, n_items - 1) // pairs, 0)),
            pl.BlockSpec((SHORT_CONV_WIDTH, 3, N_GROUPS, LANES), lambda s: (0, 0, 0, 0)),
            rep, rep,
            pl.BlockSpec((1, LANES), lambda s: (0, 0)),
        ] + [w_spec(w) for w in weights],
        out_specs=(blk(prv(0)),) + tuple(w_spec(w) for w in weights),
        scratch_shapes=[
            pltpu.VMEM((3, hps, DN_HALO + rt, LANES), F32),
            pltpu.VMEM((N_GROUPS, 3, DN_HALO, LANES), F32),
            pltpu.VMEM((hps, ts, LANES), F32),
            pltpu.VMEM((hps, ts, LANES), F32),
            pltpu.VMEM((hps, ts, LANES), BF16),
            pltpu.VMEM((N_GROUPS, 1, ts), F32),
            pltpu.VMEM((N_GROUPS, 1, ts), F32),
            pltpu.VMEM((N_GROUPS, 1, ts), F32),
            pltpu.VMEM((N_GROUPS, 1, ts), F32),
            pltpu.VMEM((N_LEVELS, CHUNK, CHUNK), BF16),
            pltpu.VMEM((n_slots, CHUNK, LANES), F32),
            pltpu.VMEM((n_slots, 2 * CHUNK, LANES), BF16),
            pltpu.VMEM((n_slots, LANES, CHUNK), BF16),
            pltpu.VMEM((n_slots, CHUNK, CHUNK), BF16),
            pltpu.VMEM((n_slots, 1, LANES), F32),
            pltpu.VMEM((hps, ts, LANES), F32),
            pltpu.VMEM((N_GROUPS, LANES, LANES), F32),
        ],
        compiler_params=pltpu.CompilerParams(
            dimension_semantics=("arbitrary",),
            vmem_limit_bytes=VMEM_LIMIT),
        name="delta",
    )(proj3, proj3, proj3, proj3, proj3, conv_w4, alog_rep, dtb_rep, norm_row, *weights)


def _mlp_kernel(x_ref, val_ref, gate_ref, bv_ref, bg_ref, dw_ref, cb_ref, lg_ref, lb_ref,
                dn_ref, wout_ref, n2_ref, wup_ref, wdn_ref, fn_ref,
                o_ref, r_scr, h_scr, halo_scr, conv_scr, *, tm, rt, blocks_per_seq):
    i = pl.program_id(0)
    f = pl.program_id(1)
    n_blocks = pl.num_programs(0) - 1
    first_tap = CONF_HALO - (CONV_WIDTH - 1)

    def conf_part():
        seq_start = (i % blocks_per_seq) == 0
        h_scr[0:CONF_HALO, :] = jnp.where(seq_start, 0.0, halo_scr[f])
        a = val_ref[0] + bv_ref[0]
        g = gate_ref[0] + bg_ref[0]
        glu = a * _sigmoid(g)
        h_scr[CONF_HALO:CONF_HALO + tm, :] = glu
        zero_glu = _tile_zero(glu)
        halo_scr[f] = h_scr[tm:tm + CONF_HALO, :]
        slot = i % 2
        zeros = []
        for r in range(tm // rt):
            base = r * rt
            acc = dw_ref[0, 0:1, :] * h_scr[base + first_tap:base + first_tap + rt, :]
            for k in range(1, CONV_WIDTH):
                lo = base + first_tap + k
                acc = acc + dw_ref[0, k:k + 1, :] * h_scr[lo:lo + rt, :]
            y = acc + cb_ref[0]
            mu = jnp.mean(y, axis=-1, keepdims=True)
            cen = y - mu
            var = jnp.mean(cen * cen, axis=-1, keepdims=True)
            hn = cen * lax.rsqrt(var + EPS) * (0.5 * lg_ref[0]) + 0.5 * lb_ref[0]
            out = hn + hn * jnp.tanh(hn)
            conv_scr[slot, f, base:base + rt, :] = out.astype(BF16)
            zeros.append(_tile_zero(out))
        return zero_glu, zeros

    def mlp_head():
        @pl.when(f == 0)
        def _():
            slot = (i + 1) % 2
            mix = jnp.concatenate([conv_scr[slot, c] for c in range(N_GROUPS)]
                                  + [dn_ref[c] for c in range(N_GROUPS)], axis=-1)
            h1 = x_ref[...] + jnp.dot(mix, wout_ref[...], preferred_element_type=F32)
            o_ref[...] = h1
            ms = jnp.mean(h1 * h1, axis=-1, keepdims=True)
            r_scr[...] = (h1 * lax.rsqrt(ms + EPS) * n2_ref[...]).astype(BF16)

    def mlp_body(zero_glu=None, zeros=None):
        m = jnp.dot(r_scr[...], wup_ref[...], preferred_element_type=F32)
        n_up = 0
        if zeros is not None:
            up_slab = 2 * LANES
            n_up = m.shape[1] // up_slab
            up_ties = [zero_glu] + zeros[:n_up - 1]
            m = jnp.concatenate(
                [jnp.concatenate([_add_tile(m[:, c * up_slab:c * up_slab + LANES], up_ties[c]),
                                  m[:, c * up_slab + LANES:(c + 1) * up_slab]], axis=1)
                 for c in range(n_up)], axis=1)
        act = jnp.square(jnp.maximum(m, 0.0)).astype(BF16)
        res = jnp.dot(act, wdn_ref[...], preferred_element_type=F32)
        if zeros is None:
            o_ref[...] += res
            return
        slab = o_ref.shape[1] // len(zeros)
        first = n_up - 1
        o_ref[:, 0:first * slab] += res[:, 0:first * slab]
        for n in range(first, len(zeros)):
            c0 = n * slab
            o_ref[:, c0 + LANES:c0 + slab] += res[:, c0 + LANES:c0 + slab]
            o_ref[:, c0:c0 + LANES] += _add_tile(res[:, c0:c0 + LANES], zeros[n])

    def mlp_tail():
        @pl.when(f == pl.num_programs(1) - 1)
        def _():
            h2 = o_ref[...]
            ms = jnp.mean(h2 * h2, axis=-1, keepdims=True)
            o_ref[...] = h2 * lax.rsqrt(ms + EPS) * fn_ref[...]

    @pl.when(i == 0)
    def _():
        @pl.when(f == 0)
        def _():
            halo_scr[...] = jnp.zeros_like(halo_scr)
        conf_part()

    @pl.when(jnp.logical_and(i > 0, i < n_blocks))
    def _():
        mlp_head()
        mlp_body(*conf_part())
        mlp_tail()

    @pl.when(i == n_blocks)
    def _():
        mlp_head()
        mlp_body()
        mlp_tail()


def _mlp(x2, proj3, b_glu, dw_w, dw_b, ln_g, ln_b, dn_o, w_out_b, n2, w_up_b, w_dn_b, fn,
         *, seq, tm=512, rt=64):
    t, d = x2.shape
    dff = w_up_b.shape[1]
    nb = t // tm
    nf = N_GROUPS
    tf = dff // nf
    prev = lambda i: jnp.maximum(i - 1, 0)
    nxt = lambda i: jnp.minimum(i, nb - 1)
    vec = lambda off: pl.BlockSpec((1, 1, LANES), lambda i, f: (off + f, 0, 0))
    return pl.pallas_call(
        functools.partial(_mlp_kernel, tm=tm, rt=rt, blocks_per_seq=seq // tm),
        out_shape=jax.ShapeDtypeStruct((t, d), F32),
        grid=(nb + 1, nf),
        in_specs=[
            pl.BlockSpec((tm, d), lambda i, f: (prev(i), 0)),
            pl.BlockSpec((1, tm, LANES), lambda i, f: (CB_VAL + f, nxt(i), 0)),
            pl.BlockSpec((1, tm, LANES), lambda i, f: (CB_GATE + f, nxt(i), 0)),
            vec(0), vec(N_GROUPS),
            pl.BlockSpec((1, CONV_WIDTH, LANES), lambda i, f: (f, 0, 0)),
            vec(0), vec(0), vec(0),
            pl.BlockSpec((N_GROUPS, tm, LANES), lambda i, f: (0, prev(i), 0)),
            pl.BlockSpec((d, d), lambda i, f: (0, 0), pipeline_mode=pl.Buffered(1)),
            pl.BlockSpec((1, d), lambda i, f: (0, 0)),
            pl.BlockSpec((d, tf), lambda i, f: (0, jnp.where(i == 0, 0, f))),
            pl.BlockSpec((tf, d), lambda i, f: (jnp.where(i == 0, 0, f), 0)),
            pl.BlockSpec((1, d), lambda i, f: (0, 0)),
        ],
        out_specs=pl.BlockSpec((tm, d), lambda i, f: (prev(i), 0)),
        scratch_shapes=[
            pltpu.VMEM((tm, d), BF16),
            pltpu.VMEM((CONF_HALO + tm, LANES), F32),
            pltpu.VMEM((N_GROUPS, CONF_HALO, LANES), F32),
            pltpu.VMEM((2, N_GROUPS, tm, LANES), BF16),
        ],
        compiler_params=pltpu.CompilerParams(
            dimension_semantics=("arbitrary", "arbitrary"),
            vmem_limit_bytes=VMEM_LIMIT),
        name="mlp",
    )(x2, proj3, proj3, b_glu, b_glu, dw_w, dw_b, ln_g, ln_b, dn_o,
      w_out_b, n2, w_up_b, w_dn_b, fn)


def _lane_rep(vals):
    return jnp.broadcast_to(vals.astype(F32)[:, None], (vals.shape[0], LANES))


def kernel(x, norm1_w, w_in, b_glu, conf_dw_w, conf_dw_b, conf_ln_g, conf_ln_b, dn_conv_w,
           dn_a_log, dn_dt_bias, dn_norm_w, w_out, norm2_w, w_mlp_up, w_mlp_down, final_norm_w):
    batch, seq, d = x.shape
    assert norm1_w.shape[0] == 1, "single-layer block"
    h = x.reshape(batch * seq, d)

    d_in = w_in.shape[2]
    w_in_p = jnp.pad(w_in[0].astype(BF16), ((0, 0), (0, N_COLBLOCKS * LANES - d_in)))
    proj3 = _inproj(h, norm1_w[0].reshape(1, d), w_in_p)

    conv_w4 = dn_conv_w[0].reshape(SHORT_CONV_WIDTH, 3, N_GROUPS, LANES)
    dn_o, w_out_b, w_up_b, w_dn_b = _delta(
        proj3, conv_w4, _lane_rep(dn_a_log[0]), _lane_rep(dn_dt_bias[0]),
        dn_norm_w[0].reshape(1, LANES), (w_out[0], w_mlp_up[0], w_mlp_down[0]),
        batch=batch, seq=seq)

    grp = lambda v: v.reshape(-1, 1, LANES)
    dw_w = conf_dw_w[0].reshape(CONV_WIDTH, N_GROUPS, LANES).transpose(1, 0, 2)
    out = _mlp(h, proj3, grp(b_glu[0]), dw_w, grp(conf_dw_b[0]), grp(conf_ln_g[0]),
               grp(conf_ln_b[0]), dn_o, w_out_b, norm2_w[0].reshape(1, d), w_up_b, w_dn_b,
               final_norm_w.reshape(1, d), seq=seq)
    return out.reshape(batch, seq, d)
```

```python
import functools

import jax
import jax.numpy as jnp
from jax import lax
from jax.experimental import pallas as pl
from jax.experimental.pallas import tpu as pltpu

F32 = jnp.float32
BF16 = jnp.bfloat16

EPS = 1e-6
LANES = 128
SUBLANES = 8
CONV_WIDTH = 31
SHORT_CONV_WIDTH = 4
N_GROUPS = 8
CONF_HALO = 32
DN_HALO = SUBLANES
CHUNK = 128
N_LEVELS = CHUNK.bit_length() - 1
HEADS_PER_STEP = 2

F_Q, F_K, F_V, F_BA = 0, 8, 16, 24
B_VAL, B_GATE, B_Z = 0, 8, 16
N_STACK = 25
GROUPS_PER_STEP = 5

VMEM_LIMIT = 56 * 1024 * 1024


def _sigmoid(x):
    return 0.5 * jnp.tanh(0.5 * x) + 0.5


def _derived_zero(x):
    bits = lax.bitcast_convert_type(x, jnp.uint32)
    bits = lax.shift_right_logical(lax.shift_right_logical(bits, jnp.uint32(16)), jnp.uint32(16))
    return lax.bitcast_convert_type(bits, F32)


def _tile_zero(x):
    return _derived_zero(jnp.max(x.reshape(-1, SUBLANES, x.shape[-1]), axis=0))


def _add_tile(x, tile):
    n = tile.shape[0]
    return jnp.concatenate([x[0:n] + tile, x[n:]], axis=0)


def _softplus(x):
    return jnp.maximum(x, 0.0) + jnp.log1p(jnp.exp(-jnp.abs(x)))


def _inproj_kernel(x_ref, nw_ref, w_ref, of_ref, ob_ref, u_scr):
    @pl.when(pl.program_id(1) == 0)
    def _():
        x = x_ref[...]
        ms = jnp.mean(x * x, axis=-1, keepdims=True)
        u_scr[...] = (x * lax.rsqrt(ms + EPS) * nw_ref[...]).astype(BF16)

    res = jnp.dot(u_scr[...], w_ref[...], preferred_element_type=F32)
    n_f = of_ref.shape[0]
    for c in range(n_f):
        of_ref[c] = res[:, c * LANES:(c + 1) * LANES]
    for c in range(ob_ref.shape[0]):
        ob_ref[c] = res[:, (n_f + c) * LANES:(n_f + c + 1) * LANES].astype(BF16)


def _inproj(x2, norm_w, w_in_p, *, tm=1024):
    t, d = x2.shape
    gps = GROUPS_PER_STEP
    tn = 2 * gps * LANES
    stack = lambda dtype: jax.ShapeDtypeStruct((N_STACK, t, LANES), dtype)
    return pl.pallas_call(
        _inproj_kernel,
        out_shape=(stack(F32), stack(BF16)),
        grid=(t // tm, N_STACK // gps),
        in_specs=[
            pl.BlockSpec((tm, d), lambda i, j: (i, 0)),
            pl.BlockSpec((1, d), lambda i, j: (0, 0)),
            pl.BlockSpec((d, tn), lambda i, j: (0, j)),
        ],
        out_specs=(pl.BlockSpec((gps, tm, LANES), lambda i, j: (j, i, 0)),
                   pl.BlockSpec((gps, tm, LANES), lambda i, j: (j, i, 0))),
        scratch_shapes=[pltpu.VMEM((tm, d), BF16)],
        compiler_params=pltpu.CompilerParams(
            dimension_semantics=("arbitrary", "arbitrary"),
            vmem_limit_bytes=VMEM_LIMIT),
        name="inproj",
    )(x2, norm_w, w_in_p)


def _delta_kernel(q_ref, k_ref, v_ref, z_ref, ba_ref, cw_ref, alog_ref, dtb_ref, nw_ref,
                  wout_ref, wup_ref, wdn_ref,
                  o_ref, wout_o, wup_o, wdn_o,
                  x_scr, halo_scr, q_scr, k_scr, v_scr,
                  beta_scr, cs_scr, f_scr, ld_scr, mask_scr,
                  u_scr, wq_scr, kgt_scr, intra_scr, ldc_scr, o_scr, s_scr,
                  *, ts, group, rt, n_items, blocks_per_seq):
    step = pl.program_id(0)
    item = jnp.minimum(step, n_items - 1)
    prev = jnp.maximum(step - 1, 0)
    pairs = N_GROUPS // HEADS_PER_STEP
    hp = item % pairs
    i = (item // pairs) % blocks_per_seq
    heads = [HEADS_PER_STEP * hp + hh for hh in range(HEADS_PER_STEP)]
    heads_prev = [HEADS_PER_STEP * (prev % pairs) + hh for hh in range(HEADS_PER_STEP)]
    nc = ts // CHUNK

    wout_o[...] = wout_ref[...].astype(BF16)
    wup_o[...] = wup_ref[...].astype(BF16)
    wdn_o[...] = wdn_ref[...].astype(BF16)

    r_io = lax.broadcasted_iota(jnp.int32, (CHUNK, CHUNK), 0)
    c_io = lax.broadcasted_iota(jnp.int32, (CHUNK, CHUNK), 1)
    causal = r_io >= c_io
    eye = jnp.where(r_io == c_io, 1.0, 0.0).astype(F32)

    @pl.when(hp == 0)
    def _():
        upper = jnp.where(r_io <= c_io, 1.0, 0.0).astype(F32)
        ones = jnp.ones((CHUNK, CHUNK), F32)
        for lvl in range(N_LEVELS):
            same = (r_io >> (lvl + 1)) == (c_io >> (lvl + 1))
            lower_left = jnp.where(((r_io >> lvl) & 1) == 1, 1.0, 0.0) * \
                jnp.where(((c_io >> lvl) & 1) == 0, 1.0, 0.0)
            mask_scr[lvl] = jnp.where(same, lower_left, 0.0).astype(BF16)
        ba_t = ba_ref[0].T
        beta = _sigmoid(ba_t[0:N_GROUPS, :])
        a_raw = ba_t[N_GROUPS:2 * N_GROUPS, :]
        neg_a = -jnp.exp(alog_ref[...])
        for c in range(nc):
            cols = slice(c * CHUNK, (c + 1) * CHUNK)
            g_c = neg_a * _softplus(a_raw[:, cols] + dtb_ref[...])
            cs = jnp.dot(g_c, upper, precision=lax.Precision.HIGHEST,
                         preferred_element_type=F32)
            tot = jnp.dot(g_c, ones, precision=lax.Precision.HIGHEST,
                          preferred_element_type=F32)
            f_c = jnp.exp(tot - cs) * beta[:, cols]
            ld_c = jnp.exp(tot)
            for hh in range(N_GROUPS):
                one = slice(hh, hh + 1)
                beta_scr[hh, :, cols] = beta[one, cols]
                cs_scr[hh, :, cols] = cs[one, :]
                f_scr[hh, :, cols] = f_c[one, :]
                ld_scr[hh, :, cols] = ld_c[one, :]

    @pl.when(i == 0)
    def _():
        for h in heads:
            halo_scr[h] = jnp.zeros((3, DN_HALO, LANES), F32)
            s_scr[h] = jnp.zeros((LANES, LANES), F32)

    set_size = HEADS_PER_STEP * group

    @pl.when(step == 0)
    def _():
        late = slice(set_size, 2 * set_size)
        u_scr[late] = jnp.zeros((set_size,) + u_scr.shape[1:], F32)
        wq_scr[late] = jnp.zeros((set_size,) + wq_scr.shape[1:], BF16)
        kgt_scr[late] = jnp.zeros((set_size,) + kgt_scr.shape[1:], BF16)
        intra_scr[late] = jnp.zeros((set_size,) + intra_scr.shape[1:], BF16)
        ldc_scr[late] = jnp.zeros((set_size,) + ldc_scr.shape[1:], F32)
        o_scr[...] = jnp.zeros_like(o_scr)

    refs = (q_ref, k_ref, v_ref)
    back = SHORT_CONV_WIDTH - 1
    for hh, h in enumerate(heads):
        for j, ref in enumerate(refs):
            x_scr[j, hh, 0:DN_HALO, :] = halo_scr[h, j]
            x_scr[j, hh, DN_HALO:DN_HALO + rt, :] = ref[hh, 0:rt, :]
            halo_scr[h, j] = ref[hh, ts - DN_HALO:ts, :]

    def conv_rows(rows, load):
        tops = []
        for hh, h in enumerate(heads):
            ys = []
            for j in range(3):
                hv = (0.5 * cw_ref[0, j, pl.ds(h, 1), :]) * load(j, hh, 0)
                for m in range(1, SHORT_CONV_WIDTH):
                    hv = hv + (0.5 * cw_ref[m, j, pl.ds(h, 1), :]) * load(j, hh, m)
                ys.append(hv + hv * jnp.tanh(hv))
            qn = ys[0] * (lax.rsqrt(jnp.sum(ys[0] * ys[0], axis=-1, keepdims=True) + EPS)
                          * (LANES ** -0.5))
            kn = ys[1] * lax.rsqrt(jnp.sum(ys[1] * ys[1], axis=-1, keepdims=True) + EPS)
            q_scr[hh, rows, :] = qn
            k_scr[hh, rows, :] = kn
            v_scr[hh, rows, :] = ys[2].astype(BF16)
            tops.append(jnp.maximum(jnp.maximum(qn, kn), ys[2]))
        return _tile_zero(functools.reduce(jnp.maximum, tops))

    conv_rows(pl.ds(0, rt),
              lambda j, hh, m: x_scr[j, hh, DN_HALO - back + m:DN_HALO - back + m + rt, :])

    def conv_tile(r, carry):
        base = pl.multiple_of(r * rt, rt)
        conv_rows(pl.ds(base, rt),
                  lambda j, hh, m: refs[j][hh, pl.ds(base - back + m, rt), :])
        return carry

    n_tiles = ts // rt
    lax.fori_loop(1, n_tiles // 2, conv_tile, 0)

    def conv_late(ties):
        for r in range(n_tiles // 2, n_tiles):
            base = r * rt
            ties.append(conv_rows(
                pl.ds(base, rt),
                lambda j, hh, m: refs[j][hh, base - back + m:base - back + m + rt, :]))
            yield
            yield
            yield

    def prep_group(first_chunk, slot0, ties=()):
        def take_ties():
            while ties:
                it = items[len(ties) % len(items)]
                it["x"] = _add_tile(it["x"], ties.pop())

        items = []
        for hh, h in enumerate(heads):
            for g in range(group):
                slot = slot0 + hh * group + g
                r0 = pl.multiple_of((first_chunk + g) * CHUNK, CHUNK)
                rows = pl.ds(r0, CHUNK)
                k_c = k_scr[hh, rows, :]
                qs_c = q_scr[hh, rows, :]
                cs_r = jnp.broadcast_to(cs_scr[h, :, rows], (CHUNK, CHUNK))
                beta_r = jnp.broadcast_to(beta_scr[h, :, rows], (CHUNK, CHUNK))
                cs_c = cs_r.T
                db = jnp.exp(jnp.where(causal, cs_c - cs_r, -jnp.inf)) * beta_r
                kq = lax.dot_general(
                    jnp.concatenate([k_c, qs_c], axis=0).astype(BF16), k_c.astype(BF16),
                    (((1,), (1,)), ((), ())), preferred_element_type=F32)
                a_mat = kq[:CHUNK] * db
                intra_scr[slot] = (kq[CHUNK:] * db).astype(BF16)
                eg = jnp.exp(cs_c)
                wq_scr[slot, CHUNK:2 * CHUNK, :] = (qs_c * eg).astype(BF16)
                kgt_scr[slot] = (k_c.T * f_scr[h, :, rows]).astype(BF16)
                ldc_scr[slot] = ld_scr[h, :, pl.ds(r0, LANES)]
                rhs = jnp.concatenate([v_scr[hh, rows, :], (k_c * eg).astype(BF16)], axis=1)
                items.append(dict(a=a_mat.astype(BF16), x=eye - a_mat * mask_scr[0].astype(F32),
                                  rhs=rhs, slot=slot))
        yield
        for lvl in range(1, N_LEVELS):
            take_ties()
            for it in items:
                it["xb"] = it["x"].astype(BF16)
                it["mx"] = jnp.dot(it["a"] * mask_scr[lvl], it["xb"], preferred_element_type=F32)
            yield
            for it in items:
                it["x"] = it["x"] - jnp.dot(it["xb"], it["mx"].astype(BF16),
                                            preferred_element_type=F32)
            yield
        take_ties()
        for it in items:
            uw = jnp.dot(it["x"].astype(BF16), it["rhs"], preferred_element_type=F32)
            u_scr[it["slot"]] = uw[:, :LANES]
            wq_scr[it["slot"], 0:CHUNK, :] = uw[:, LANES:].astype(BF16)
        yield

    def step_group(first_chunk, slot0, states, ties=()):
        for g in range(group):
            r0 = pl.multiple_of((first_chunk + g) * CHUNK, CHUNK)
            rows = pl.ds(r0, CHUNK)
            ws = [jnp.dot(wq_scr[slot0 + hh * group + g], states[hh].astype(BF16),
                          preferred_element_type=F32) for hh in range(HEADS_PER_STEP)]
            yield
            for hh in range(HEADS_PER_STEP):
                slot = slot0 + hh * group + g
                v_new = (u_scr[slot] - ws[hh][:CHUNK]).astype(BF16)
                o_c = ws[hh][CHUNK:] + jnp.dot(intra_scr[slot], v_new,
                                               preferred_element_type=F32)
                if ties:
                    o_c = _add_tile(o_c, ties.pop())
                o_scr[hh, rows, :] = o_c
                states[hh] = states[hh] * ldc_scr[slot] + jnp.dot(
                    kgt_scr[slot], v_new, preferred_element_type=F32)
            yield

    def interleave(*gens):
        live = list(gens)
        while live:
            for gen in list(live):
                try:
                    next(gen)
                except StopIteration:
                    live.remove(gen)

    assert nc == 2 * group and n_tiles % 2 == 0

    def out_rows(rows):
        tops = []
        for hh in range(HEADS_PER_STEP):
            o = o_scr[hh, rows, :]
            o = o * lax.rsqrt(jnp.mean(o * o, axis=-1, keepdims=True) + EPS) * nw_ref[...]
            hz = 0.5 * z_ref[hh, rows, :].astype(F32)
            res = o * (hz + hz * jnp.tanh(hz))
            o_ref[hh, rows, :] = res.astype(o_ref.dtype)
            tops.append(res)
        return _tile_zero(functools.reduce(jnp.maximum, tops))

    def out_tiles(ties):
        for r in range(n_tiles):
            ties.append(out_rows(pl.ds(r * rt, rt)))
            for _ in range(2 * group // n_tiles):
                yield

    conv_ties = []
    states = [s_scr[h] for h in heads_prev]
    interleave(conv_late(conv_ties), prep_group(0, 0, conv_ties),
               step_group(group, set_size, states))
    for hh, h in enumerate(heads_prev):
        s_scr[h] = states[hh]

    out_ties = []
    states = [s_scr[h] for h in heads]
    interleave(out_tiles(out_ties), step_group(0, 0, states, out_ties),
               prep_group(group, set_size))
    for hh, h in enumerate(heads):
        s_scr[h] = states[hh]


def _delta(proj_f, proj_b, conv_w4, alog_rep, dtb_rep, norm_row, weights, *, batch, seq,
           ts=2048, group=8, rt=256):
    t = proj_f.shape[1]
    ns = seq // ts
    hps = HEADS_PER_STEP
    pairs = N_GROUPS // hps
    n_items = batch * ns * pairs
    for w in weights:
        assert w.shape[0] % n_items == 0 and (w.shape[0] // n_items) % 16 == 0, w.shape
    w_spec = lambda w: pl.BlockSpec((w.shape[0] // n_items, w.shape[1]),
                                    lambda s: (jnp.minimum(s, n_items - 1), 0))

    def cur(off):
        def index_map(s):
            it = jnp.minimum(s, n_items - 1)
            return (off // hps + it % pairs, it // pairs, 0)
        return index_map

    def prv(off):
        def index_map(s):
            it = jnp.maximum(s - 1, 0)
            return (off // hps + it % pairs, it // pairs, 0)
        return index_map

    blk = lambda index_map: pl.BlockSpec((hps, ts, LANES), index_map)
    rep = pl.BlockSpec((N_GROUPS, LANES), lambda s: (0, 0))
    n_slots = 2 * hps * group
    return pl.pallas_call(
        functools.partial(_delta_kernel, ts=ts, group=group, rt=rt, n_items=n_items,
                          blocks_per_seq=ns),
        out_shape=(jax.ShapeDtypeStruct((N_GROUPS, t, LANES), BF16),)
        + tuple(jax.ShapeDtypeStruct(w.shape, BF16) for w in weights),
        grid=(n_items + 1,),
        in_specs=[
            blk(cur(F_Q)), blk(cur(F_K)), blk(cur(F_V)), blk(prv(B_Z)),
            pl.BlockSpec((1, ts, LANES),
                         lambda s: (F_BA, jnp.minimum(s, n_items - 1) // pairs, 0)),
            pl.BlockSpec((SHORT_CONV_WIDTH, 3, N_GROUPS, LANES), lambda s: (0, 0, 0, 0)),
            rep, rep,
            pl.BlockSpec((1, LANES), lambda s: (0, 0)),
        ] + [w_spec(w) for w in weights],
        out_specs=(blk(prv(0)),) + tuple(w_spec(w) for w in weights),
        scratch_shapes=[
            pltpu.VMEM((3, hps, DN_HALO + rt, LANES), F32),
            pltpu.VMEM((N_GROUPS, 3, DN_HALO, LANES), F32),
            pltpu.VMEM((hps, ts, LANES), F32),
            pltpu.VMEM((hps, ts, LANES), F32),
            pltpu.VMEM((hps, ts, LANES), BF16),
            pltpu.VMEM((N_GROUPS, 1, ts), F32),
            pltpu.VMEM((N_GROUPS, 1, ts), F32),
            pltpu.VMEM((N_GROUPS, 1, ts), F32),
            pltpu.VMEM((N_GROUPS, 1, ts), F32),
            pltpu.VMEM((N_LEVELS, CHUNK, CHUNK), BF16),
            pltpu.VMEM((n_slots, CHUNK, LANES), F32),
            pltpu.VMEM((n_slots, 2 * CHUNK, LANES), BF16),
            pltpu.VMEM((n_slots, LANES, CHUNK), BF16),
            pltpu.VMEM((n_slots, CHUNK, CHUNK), BF16),
            pltpu.VMEM((n_slots, 1, LANES), F32),
            pltpu.VMEM((hps, ts, LANES), F32),
            pltpu.VMEM((N_GROUPS, LANES, LANES), F32),
        ],
        compiler_params=pltpu.CompilerParams(
            dimension_semantics=("arbitrary",),
            vmem_limit_bytes=VMEM_LIMIT),
        name="delta",
    )(proj_f, proj_f, proj_f, proj_b, proj_f, conv_w4, alog_rep, dtb_rep, norm_row, *weights)


def _mlp_kernel(x_ref, val_ref, gate_ref, bv_ref, bg_ref, dw_ref, cb_ref, lg_ref, lb_ref,
                dn_ref, wout_ref, n2_ref, wup_ref, wdn_ref, fn_ref,
                o_ref, r_scr, h_scr, halo_scr, conv_scr, *, tm, rt, blocks_per_seq):
    i = pl.program_id(0)
    f = pl.program_id(1)
    n_blocks = pl.num_programs(0) - 1
    first_tap = CONF_HALO - (CONV_WIDTH - 1)

    def conf_part():
        seq_start = (i % blocks_per_seq) == 0
        h_scr[0:CONF_HALO, :] = jnp.where(seq_start, 0.0, halo_scr[f])
        a = val_ref[0].astype(F32) + bv_ref[0]
        g = gate_ref[0].astype(F32) + bg_ref[0]
        glu = a * _sigmoid(g)
        h_scr[CONF_HALO:CONF_HALO + tm, :] = glu
        zero_glu = _tile_zero(glu)
        halo_scr[f] = h_scr[tm:tm + CONF_HALO, :]
        slot = i % 2
        zeros = []
        for r in range(tm // rt):
            base = r * rt
            acc = dw_ref[0, 0:1, :] * h_scr[base + first_tap:base + first_tap + rt, :]
            for k in range(1, CONV_WIDTH):
                lo = base + first_tap + k
                acc = acc + dw_ref[0, k:k + 1, :] * h_scr[lo:lo + rt, :]
            y = acc + cb_ref[0]
            mu = jnp.mean(y, axis=-1, keepdims=True)
            cen = y - mu
            var = jnp.mean(cen * cen, axis=-1, keepdims=True)
            hn = cen * lax.rsqrt(var + EPS) * (0.5 * lg_ref[0]) + 0.5 * lb_ref[0]
            out = hn + hn * jnp.tanh(hn)
            conv_scr[slot, f, base:base + rt, :] = out.astype(BF16)
            zeros.append(_tile_zero(out))
        return zero_glu, zeros

    def mlp_head():
        @pl.when(f == 0)
        def _():
            slot = (i + 1) % 2
            mix = jnp.concatenate([conv_scr[slot, c] for c in range(N_GROUPS)]
                                  + [dn_ref[c] for c in range(N_GROUPS)], axis=-1)
            h1 = x_ref[...] + jnp.dot(mix, wout_ref[...], preferred_element_type=F32)
            o_ref[...] = h1
            ms = jnp.mean(h1 * h1, axis=-1, keepdims=True)
            r_scr[...] = (h1 * lax.rsqrt(ms + EPS) * n2_ref[...]).astype(BF16)

    def mlp_body(zero_glu=None, zeros=None):
        m = jnp.dot(r_scr[...], wup_ref[...], preferred_element_type=F32)
        n_up = 0
        if zeros is not None:
            up_slab = 2 * LANES
            n_up = m.shape[1] // up_slab
            up_ties = [zero_glu] + zeros[:n_up - 1]
            m = jnp.concatenate(
                [jnp.concatenate([_add_tile(m[:, c * up_slab:c * up_slab + LANES], up_ties[c]),
                                  m[:, c * up_slab + LANES:(c + 1) * up_slab]], axis=1)
                 for c in range(n_up)], axis=1)
        act = jnp.square(jnp.maximum(m, 0.0)).astype(BF16)
        res = jnp.dot(act, wdn_ref[...], preferred_element_type=F32)
        if zeros is None:
            o_ref[...] += res
            return
        slab = o_ref.shape[1] // len(zeros)
        first = n_up - 1
        o_ref[:, 0:first * slab] += res[:, 0:first * slab]
        for n in range(first, len(zeros)):
            c0 = n * slab
            o_ref[:, c0 + LANES:c0 + slab] += res[:, c0 + LANES:c0 + slab]
            o_ref[:, c0:c0 + LANES] += _add_tile(res[:, c0:c0 + LANES], zeros[n])

    def mlp_tail():
        @pl.when(f == pl.num_programs(1) - 1)
        def _():
            h2 = o_ref[...]
            ms = jnp.mean(h2 * h2, axis=-1, keepdims=True)
            o_ref[...] = h2 * lax.rsqrt(ms + EPS) * fn_ref[...]

    @pl.when(i == 0)
    def _():
        @pl.when(f == 0)
        def _():
            halo_scr[...] = jnp.zeros_like(halo_scr)
        conf_part()

    @pl.when(jnp.logical_and(i > 0, i < n_blocks))
    def _():
        mlp_head()
        mlp_body(*conf_part())
        mlp_tail()

    @pl.when(i == n_blocks)
    def _():
        mlp_head()
        mlp_body()
        mlp_tail()


def _mlp(x2, proj_b, b_glu, dw_w, dw_b, ln_g, ln_b, dn_o, w_out_b, n2, w_up_b, w_dn_b, fn,
         *, seq, tm=512, rt=64):
    t, d = x2.shape
    dff = w_up_b.shape[1]
    nb = t // tm
    nf = N_GROUPS
    tf = dff // nf
    prev = lambda i: jnp.maximum(i - 1, 0)
    nxt = lambda i: jnp.minimum(i, nb - 1)
    vec = lambda off: pl.BlockSpec((1, 1, LANES), lambda i, f: (off + f, 0, 0))
    return pl.pallas_call(
        functools.partial(_mlp_kernel, tm=tm, rt=rt, blocks_per_seq=seq // tm),
        out_shape=jax.ShapeDtypeStruct((t, d), F32),
        grid=(nb + 1, nf),
        in_specs=[
            pl.BlockSpec((tm, d), lambda i, f: (prev(i), 0)),
            pl.BlockSpec((1, tm, LANES), lambda i, f: (B_VAL + f, nxt(i), 0)),
            pl.BlockSpec((1, tm, LANES), lambda i, f: (B_GATE + f, nxt(i), 0)),
            vec(0), vec(N_GROUPS),
            pl.BlockSpec((1, CONV_WIDTH, LANES), lambda i, f: (f, 0, 0)),
            vec(0), vec(0), vec(0),
            pl.BlockSpec((N_GROUPS, tm, LANES), lambda i, f: (0, prev(i), 0)),
            pl.BlockSpec((d, d), lambda i, f: (0, 0), pipeline_mode=pl.Buffered(1)),
            pl.BlockSpec((1, d), lambda i, f: (0, 0)),
            pl.BlockSpec((d, tf), lambda i, f: (0, jnp.where(i == 0, 0, f))),
            pl.BlockSpec((tf, d), lambda i, f: (jnp.where(i == 0, 0, f), 0)),
            pl.BlockSpec((1, d), lambda i, f: (0, 0)),
        ],
        out_specs=pl.BlockSpec((tm, d), lambda i, f: (prev(i), 0)),
        scratch_shapes=[
            pltpu.VMEM((tm, d), BF16),
            pltpu.VMEM((CONF_HALO + tm, LANES), F32),
            pltpu.VMEM((N_GROUPS, CONF_HALO, LANES), F32),
            pltpu.VMEM((2, N_GROUPS, tm, LANES), BF16),
        ],
        compiler_params=pltpu.CompilerParams(
            dimension_semantics=("arbitrary", "arbitrary"),
            vmem_limit_bytes=VMEM_LIMIT),
        name="mlp",
    )(x2, proj_b, proj_b, b_glu, b_glu, dw_w, dw_b, ln_g, ln_b, dn_o,
      w_out_b, n2, w_up_b, w_dn_b, fn)


def _lane_rep(vals):
    return jnp.broadcast_to(vals.astype(F32)[:, None], (vals.shape[0], LANES))


def kernel(x, norm1_w, w_in, b_glu, conf_dw_w, conf_dw_b, conf_ln_g, conf_ln_b, dn_conv_w,
           dn_a_log, dn_dt_bias, dn_norm_w, w_out, norm2_w, w_mlp_up, w_mlp_down, final_norm_w):
    batch, seq, d = x.shape
    assert norm1_w.shape[0] == 1, "single-layer block"
    h = x.reshape(batch * seq, d)

    w = w_in[0].astype(BF16)
    grp_cols = lambda role, g: w[:, (role * N_GROUPS + g) * LANES:(role * N_GROUPS + g + 1) * LANES]
    tail = jnp.pad(w[:, 6 * N_GROUPS * LANES:], ((0, 0), (0, LANES - (w.shape[1] - 6 * N_GROUPS * LANES))))
    f_groups = [grp_cols(role, g) for role in (2, 3, 4) for g in range(N_GROUPS)] + [tail]
    b_groups = ([grp_cols(role, g) for role in (0, 1, 5) for g in range(N_GROUPS)]
                + [jnp.zeros((d, LANES), BF16)])
    gps = GROUPS_PER_STEP
    w_in_p = jnp.concatenate(
        [blk for j in range(N_STACK // gps)
         for blk in f_groups[j * gps:(j + 1) * gps] + b_groups[j * gps:(j + 1) * gps]], axis=1)
    proj_f, proj_b = _inproj(h, norm1_w[0].reshape(1, d), w_in_p)

    conv_w4 = dn_conv_w[0].reshape(SHORT_CONV_WIDTH, 3, N_GROUPS, LANES)
    dn_o, w_out_b, w_up_b, w_dn_b = _delta(
        proj_f, proj_b, conv_w4, _lane_rep(dn_a_log[0]), _lane_rep(dn_dt_bias[0]),
        dn_norm_w[0].reshape(1, LANES), (w_out[0], w_mlp_up[0], w_mlp_down[0]),
        batch=batch, seq=seq)

    grp = lambda v: v.reshape(-1, 1, LANES)
    dw_w = conf_dw_w[0].reshape(CONV_WIDTH, N_GROUPS, LANES).transpose(1, 0, 2)
    out = _mlp(h, proj_b, grp(b_glu[0]), dw_w, grp(conf_dw_b[0]), grp(conf_ln_g[0]),
               grp(conf_ln_b[0]), dn_o, w_out_b, norm2_w[0].reshape(1, d), w_up_b, w_dn_b,
               final_norm_w.reshape(1, d), seq=seq)
    return out.reshape(batch, seq, d)
```

```python
import functools

import jax
import jax.numpy as jnp
from jax import lax
from jax.experimental import pallas as pl
from jax.experimental.pallas import tpu as pltpu

F32 = jnp.float32
BF16 = jnp.bfloat16

EPS = 1e-6
LANES = 128
SUBLANES = 8
CONV_WIDTH = 31
SHORT_CONV_WIDTH = 4
N_GROUPS = 8
CONF_HALO = 32
DN_HALO = SUBLANES
CHUNK = 128
N_LEVELS = CHUNK.bit_length() - 1
HEADS_PER_STEP = 2

CB_VAL, CB_GATE, CB_Q, CB_K, CB_V, CB_Z, CB_BA = 0, 8, 16, 24, 32, 40, 48
N_COLBLOCKS = 50

VMEM_LIMIT = 56 * 1024 * 1024


def _sigmoid(x):
    return 0.5 * jnp.tanh(0.5 * x) + 0.5


def _derived_zero(x):
    bits = lax.bitcast_convert_type(x, jnp.uint32)
    bits = lax.shift_right_logical(lax.shift_right_logical(bits, jnp.uint32(16)), jnp.uint32(16))
    return lax.bitcast_convert_type(bits, F32)


def _tile_zero(x):
    return _derived_zero(jnp.max(x.reshape(-1, SUBLANES, x.shape[-1]), axis=0))


def _add_tile(x, tile):
    n = tile.shape[0]
    return jnp.concatenate([x[0:n] + tile, x[n:]], axis=0)


def _softplus(x):
    return jnp.maximum(x, 0.0) + jnp.log1p(jnp.exp(-jnp.abs(x)))


def _wprep_kernel(w_ref, o_ref):
    n = w_ref.shape[1]
    o_ref[:, 0:n] = w_ref[...].astype(BF16)
    o_ref[:, n:] = jnp.zeros((o_ref.shape[0], o_ref.shape[1] - n), BF16)


def _wprep(w, n_pad, *, rows=64):
    d, n = w.shape
    return pl.pallas_call(
        _wprep_kernel,
        out_shape=jax.ShapeDtypeStruct((d, n_pad), BF16),
        grid=(d // rows,),
        in_specs=[pl.BlockSpec((rows, n), lambda i: (i, 0))],
        out_specs=pl.BlockSpec((rows, n_pad), lambda i: (i, 0)),
        compiler_params=pltpu.CompilerParams(dimension_semantics=("arbitrary",)),
        name="wprep",
    )(w)


def _inproj_kernel(x_ref, nw_ref, w_ref, o_ref, u_scr):
    @pl.when(pl.program_id(1) == 0)
    def _():
        x = x_ref[...]
        ms = jnp.mean(x * x, axis=-1, keepdims=True)
        u_scr[...] = (x * lax.rsqrt(ms + EPS) * nw_ref[...]).astype(BF16)

    res = jnp.dot(u_scr[...], w_ref[...], preferred_element_type=F32)
    for c in range(o_ref.shape[0]):
        o_ref[c] = res[:, c * LANES:(c + 1) * LANES]


def _inproj(x2, norm_w, w_in_p, *, tm=1024, tn=1280):
    t, d = x2.shape
    n = w_in_p.shape[1]
    return pl.pallas_call(
        _inproj_kernel,
        out_shape=jax.ShapeDtypeStruct((n // LANES, t, LANES), F32),
        grid=(t // tm, n // tn),
        in_specs=[
            pl.BlockSpec((tm, d), lambda i, j: (i, 0)),
            pl.BlockSpec((1, d), lambda i, j: (0, 0)),
            pl.BlockSpec((d, tn), lambda i, j: (0, j)),
        ],
        out_specs=pl.BlockSpec((tn // LANES, tm, LANES), lambda i, j: (j, i, 0)),
        scratch_shapes=[pltpu.VMEM((tm, d), BF16)],
        compiler_params=pltpu.CompilerParams(
            dimension_semantics=("arbitrary", "arbitrary"),
            vmem_limit_bytes=VMEM_LIMIT),
        name="inproj",
    )(x2, norm_w, w_in_p)


def _delta_kernel(q_ref, k_ref, v_ref, z_ref, ba_ref, cw_ref, alog_ref, dtb_ref, nw_ref,
                  wout_ref, wup_ref, wdn_ref,
                  o_ref, wout_o, wup_o, wdn_o,
                  x_scr, halo_scr, q_scr, k_scr, v_scr,
                  beta_scr, cs_scr, f_scr, ld_scr, mask_scr,
                  u_scr, wq_scr, kgt_scr, intra_scr, ldc_scr, o_scr, s_scr,
                  *, ts, group, rt, n_items, blocks_per_seq):
    step = pl.program_id(0)
    item = jnp.minimum(step, n_items - 1)
    prev = jnp.maximum(step - 1, 0)
    pairs = N_GROUPS // HEADS_PER_STEP
    hp = item % pairs
    i = (item // pairs) % blocks_per_seq
    heads = [HEADS_PER_STEP * hp + hh for hh in range(HEADS_PER_STEP)]
    heads_prev = [HEADS_PER_STEP * (prev % pairs) + hh for hh in range(HEADS_PER_STEP)]
    nc = ts // CHUNK

    wout_o[...] = wout_ref[...].astype(BF16)
    wup_o[...] = wup_ref[...].astype(BF16)
    wdn_o[...] = wdn_ref[...].astype(BF16)

    r_io = lax.broadcasted_iota(jnp.int32, (CHUNK, CHUNK), 0)
    c_io = lax.broadcasted_iota(jnp.int32, (CHUNK, CHUNK), 1)
    causal = r_io >= c_io
    eye = jnp.where(r_io == c_io, 1.0, 0.0).astype(F32)

    @pl.when(hp == 0)
    def _():
        upper = jnp.where(r_io <= c_io, 1.0, 0.0).astype(F32)
        ones = jnp.ones((CHUNK, CHUNK), F32)
        for lvl in range(N_LEVELS):
            same = (r_io >> (lvl + 1)) == (c_io >> (lvl + 1))
            lower_left = jnp.where(((r_io >> lvl) & 1) == 1, 1.0, 0.0) * \
                jnp.where(((c_io >> lvl) & 1) == 0, 1.0, 0.0)
            mask_scr[lvl] = jnp.where(same, lower_left, 0.0).astype(BF16)
        ba_t = ba_ref[0].T
        beta = _sigmoid(ba_t[0:N_GROUPS, :])
        a_raw = ba_t[N_GROUPS:2 * N_GROUPS, :]
        neg_a = -jnp.exp(alog_ref[...])
        for c in range(nc):
            cols = slice(c * CHUNK, (c + 1) * CHUNK)
            g_c = neg_a * _softplus(a_raw[:, cols] + dtb_ref[...])
            cs = jnp.dot(g_c, upper, precision=lax.Precision.HIGHEST,
                         preferred_element_type=F32)
            tot = jnp.dot(g_c, ones, precision=lax.Precision.HIGHEST,
                          preferred_element_type=F32)
            f_c = jnp.exp(tot - cs) * beta[:, cols]
            ld_c = jnp.exp(tot)
            for hh in range(N_GROUPS):
                one = slice(hh, hh + 1)
                beta_scr[hh, :, cols] = beta[one, cols]
                cs_scr[hh, :, cols] = cs[one, :]
                f_scr[hh, :, cols] = f_c[one, :]
                ld_scr[hh, :, cols] = ld_c[one, :]

    @pl.when(i == 0)
    def _():
        for h in heads:
            halo_scr[h] = jnp.zeros((3, DN_HALO, LANES), F32)
            s_scr[h] = jnp.zeros((LANES, LANES), F32)

    set_size = HEADS_PER_STEP * group

    @pl.when(step == 0)
    def _():
        late = slice(set_size, 2 * set_size)
        u_scr[late] = jnp.zeros((set_size,) + u_scr.shape[1:], F32)
        wq_scr[late] = jnp.zeros((set_size,) + wq_scr.shape[1:], BF16)
        kgt_scr[late] = jnp.zeros((set_size,) + kgt_scr.shape[1:], BF16)
        intra_scr[late] = jnp.zeros((set_size,) + intra_scr.shape[1:], BF16)
        ldc_scr[late] = jnp.zeros((set_size,) + ldc_scr.shape[1:], F32)
        o_scr[...] = jnp.zeros_like(o_scr)

    refs = (q_ref, k_ref, v_ref)
    back = SHORT_CONV_WIDTH - 1
    for hh, h in enumerate(heads):
        for j, ref in enumerate(refs):
            x_scr[j, hh, 0:DN_HALO, :] = halo_scr[h, j]
            x_scr[j, hh, DN_HALO:DN_HALO + rt, :] = ref[hh, 0:rt, :]
            halo_scr[h, j] = ref[hh, ts - DN_HALO:ts, :]

    def conv_rows(rows, load):
        tops = []
        for hh, h in enumerate(heads):
            ys = []
            for j in range(3):
                hv = (0.5 * cw_ref[0, j, pl.ds(h, 1), :]) * load(j, hh, 0)
                for m in range(1, SHORT_CONV_WIDTH):
                    hv = hv + (0.5 * cw_ref[m, j, pl.ds(h, 1), :]) * load(j, hh, m)
                ys.append(hv + hv * jnp.tanh(hv))
            qn = ys[0] * (lax.rsqrt(jnp.sum(ys[0] * ys[0], axis=-1, keepdims=True) + EPS)
                          * (LANES ** -0.5))
            kn = ys[1] * lax.rsqrt(jnp.sum(ys[1] * ys[1], axis=-1, keepdims=True) + EPS)
            q_scr[hh, rows, :] = qn
            k_scr[hh, rows, :] = kn
            v_scr[hh, rows, :] = ys[2].astype(BF16)
            tops.append(jnp.maximum(jnp.maximum(qn, kn), ys[2]))
        return _tile_zero(functools.reduce(jnp.maximum, tops))

    conv_rows(pl.ds(0, rt),
              lambda j, hh, m: x_scr[j, hh, DN_HALO - back + m:DN_HALO - back + m + rt, :])

    def conv_tile(r, carry):
        base = pl.multiple_of(r * rt, rt)
        conv_rows(pl.ds(base, rt),
                  lambda j, hh, m: refs[j][hh, pl.ds(base - back + m, rt), :])
        return carry

    n_tiles = ts // rt
    lax.fori_loop(1, n_tiles // 2, conv_tile, 0)

    def conv_late(ties):
        for r in range(n_tiles // 2, n_tiles):
            base = r * rt
            ties.append(conv_rows(
                pl.ds(base, rt),
                lambda j, hh, m: refs[j][hh, base - back + m:base - back + m + rt, :]))
            yield
            yield
            yield

    def prep_group(first_chunk, slot0, ties=()):
        def take_ties():
            while ties:
                it = items[len(ties) % len(items)]
                it["x"] = _add_tile(it["x"], ties.pop())

        items = []
        for hh, h in enumerate(heads):
            for g in range(group):
                slot = slot0 + hh * group + g
                r0 = pl.multiple_of((first_chunk + g) * CHUNK, CHUNK)
                rows = pl.ds(r0, CHUNK)
                k_c = k_scr[hh, rows, :]
                qs_c = q_scr[hh, rows, :]
                cs_r = jnp.broadcast_to(cs_scr[h, :, rows], (CHUNK, CHUNK))
                beta_r = jnp.broadcast_to(beta_scr[h, :, rows], (CHUNK, CHUNK))
                cs_c = cs_r.T
                db = jnp.exp(jnp.where(causal, cs_c - cs_r, -jnp.inf)) * beta_r
                kq = lax.dot_general(
                    jnp.concatenate([k_c, qs_c], axis=0).astype(BF16), k_c.astype(BF16),
                    (((1,), (1,)), ((), ())), preferred_element_type=F32)
                a_mat = kq[:CHUNK] * db
                intra_scr[slot] = (kq[CHUNK:] * db).astype(BF16)
                eg = jnp.exp(cs_c)
                wq_scr[slot, CHUNK:2 * CHUNK, :] = (qs_c * eg).astype(BF16)
                kgt_scr[slot] = (k_c.T * f_scr[h, :, rows]).astype(BF16)
                ldc_scr[slot] = ld_scr[h, :, pl.ds(r0, LANES)]
                rhs = jnp.concatenate([v_scr[hh, rows, :], (k_c * eg).astype(BF16)], axis=1)
                items.append(dict(a=a_mat.astype(BF16), x=eye - a_mat * mask_scr[0].astype(F32),
                                  rhs=rhs, slot=slot))
        yield
        for lvl in range(1, N_LEVELS):
            take_ties()
            for it in items:
                it["xb"] = it["x"].astype(BF16)
                it["mx"] = jnp.dot(it["a"] * mask_scr[lvl], it["xb"], preferred_element_type=F32)
            yield
            for it in items:
                it["x"] = it["x"] - jnp.dot(it["xb"], it["mx"].astype(BF16),
                                            preferred_element_type=F32)
            yield
        take_ties()
        for it in items:
            uw = jnp.dot(it["x"].astype(BF16), it["rhs"], preferred_element_type=F32)
            u_scr[it["slot"]] = uw[:, :LANES]
            wq_scr[it["slot"], 0:CHUNK, :] = uw[:, LANES:].astype(BF16)
        yield

    def step_group(first_chunk, slot0, states, ties=()):
        for g in range(group):
            r0 = pl.multiple_of((first_chunk + g) * CHUNK, CHUNK)
            rows = pl.ds(r0, CHUNK)
            ws = [jnp.dot(wq_scr[slot0 + hh * group + g], states[hh].astype(BF16),
                          preferred_element_type=F32) for hh in range(HEADS_PER_STEP)]
            yield
            for hh in range(HEADS_PER_STEP):
                slot = slot0 + hh * group + g
                v_new = (u_scr[slot] - ws[hh][:CHUNK]).astype(BF16)
                o_c = ws[hh][CHUNK:] + jnp.dot(intra_scr[slot], v_new,
                                               preferred_element_type=F32)
                if ties:
                    o_c = _add_tile(o_c, ties.pop())
                o_scr[hh, rows, :] = o_c
                states[hh] = states[hh] * ldc_scr[slot] + jnp.dot(
                    kgt_scr[slot], v_new, preferred_element_type=F32)
            yield

    def interleave(*gens):
        live = list(gens)
        while live:
            for gen in list(live):
                try:
                    next(gen)
                except StopIteration:
                    live.remove(gen)

    assert nc == 2 * group and n_tiles % 2 == 0

    def out_rows(rows):
        tops = []
        for hh in range(HEADS_PER_STEP):
            o = o_scr[hh, rows, :]
            o = o * lax.rsqrt(jnp.mean(o * o, axis=-1, keepdims=True) + EPS) * nw_ref[...]
            hz = 0.5 * z_ref[hh, rows, :]
            res = o * (hz + hz * jnp.tanh(hz))
            o_ref[hh, rows, :] = res.astype(o_ref.dtype)
            tops.append(res)
        return _tile_zero(functools.reduce(jnp.maximum, tops))

    def out_tiles(ties):
        for r in range(n_tiles):
            ties.append(out_rows(pl.ds(r * rt, rt)))
            for _ in range(2 * group // n_tiles):
                yield

    conv_ties = []
    states = [s_scr[h] for h in heads_prev]
    interleave(conv_late(conv_ties), prep_group(0, 0, conv_ties),
               step_group(group, set_size, states))
    for hh, h in enumerate(heads_prev):
        s_scr[h] = states[hh]

    out_ties = []
    states = [s_scr[h] for h in heads]
    interleave(out_tiles(out_ties), step_group(0, 0, states, out_ties),
               prep_group(group, set_size))
    for hh, h in enumerate(heads):
        s_scr[h] = states[hh]


def _delta(proj3, conv_w4, alog_rep, dtb_rep, norm_row, weights, *, batch, seq,
           ts=2048, group=8, rt=256):
    t = proj3.shape[1]
    ns = seq // ts
    hps = HEADS_PER_STEP
    pairs = N_GROUPS // hps
    n_items = batch * ns * pairs
    for w in weights:
        assert w.shape[0] % n_items == 0 and (w.shape[0] // n_items) % 16 == 0, w.shape
    w_spec = lambda w: pl.BlockSpec((w.shape[0] // n_items, w.shape[1]),
                                    lambda s: (jnp.minimum(s, n_items - 1), 0))

    def cur(off):
        def index_map(s):
            it = jnp.minimum(s, n_items - 1)
            return (off // hps + it % pairs, it // pairs, 0)
        return index_map

    def prv(off):
        def index_map(s):
            it = jnp.maximum(s - 1, 0)
            return (off // hps + it % pairs, it // pairs, 0)
        return index_map

    blk = lambda index_map: pl.BlockSpec((hps, ts, LANES), index_map)
    rep = pl.BlockSpec((N_GROUPS, LANES), lambda s: (0, 0))
    n_slots = 2 * hps * group
    return pl.pallas_call(
        functools.partial(_delta_kernel, ts=ts, group=group, rt=rt, n_items=n_items,
                          blocks_per_seq=ns),
        out_shape=(jax.ShapeDtypeStruct((N_GROUPS, t, LANES), BF16),)
        + tuple(jax.ShapeDtypeStruct(w.shape, BF16) for w in weights),
        grid=(n_items + 1,),
        in_specs=[
            blk(cur(CB_Q)), blk(cur(CB_K)), blk(cur(CB_V)), blk(prv(CB_Z)),
            pl.BlockSpec((1, ts, LANES),
                         lambda s: (CB_BA, jnp.minimum(s, n_items - 1) // pairs, 0)),
            pl.BlockSpec((SHORT_CONV_WIDTH, 3, N_GROUPS, LANES), lambda s: (0, 0, 0, 0)),
            rep, rep,
            pl.BlockSpec((1, LANES), lambda s: (0, 0)),
        ] + [w_spec(w) for w in weights],
        out_specs=(blk(prv(0)),) + tuple(w_spec(w) for w in weights),
        scratch_shapes=[
            pltpu.VMEM((3, hps, DN_HALO + rt, LANES), F32),
            pltpu.VMEM((N_GROUPS, 3, DN_HALO, LANES), F32),
            pltpu.VMEM((hps, ts, LANES), F32),
            pltpu.VMEM((hps, ts, LANES), F32),
            pltpu.VMEM((hps, ts, LANES), BF16),
            pltpu.VMEM((N_GROUPS, 1, ts), F32),
            pltpu.VMEM((N_GROUPS, 1, ts), F32),
            pltpu.VMEM((N_GROUPS, 1, ts), F32),
            pltpu.VMEM((N_GROUPS, 1, ts), F32),
            pltpu.VMEM((N_LEVELS, CHUNK, CHUNK), BF16),
            pltpu.VMEM((n_slots, CHUNK, LANES), F32),
            pltpu.VMEM((n_slots, 2 * CHUNK, LANES), BF16),
            pltpu.VMEM((n_slots, LANES, CHUNK), BF16),
            pltpu.VMEM((n_slots, CHUNK, CHUNK), BF16),
            pltpu.VMEM((n_slots, 1, LANES), F32),
            pltpu.VMEM((hps, ts, LANES), F32),
            pltpu.VMEM((N_GROUPS, LANES, LANES), F32),
        ],
        compiler_params=pltpu.CompilerParams(
            dimension_semantics=("arbitrary",),
            vmem_limit_bytes=VMEM_LIMIT),
        name="delta",
    )(proj3, proj3, proj3, proj3, proj3, conv_w4, alog_rep, dtb_rep, norm_row, *weights)


def _mlp_kernel(x_ref, val_ref, gate_ref, bv_ref, bg_ref, dw_ref, cb_ref, lg_ref, lb_ref,
                dn_ref, wout_ref, n2_ref, wup_ref, wdn_ref, fn_ref,
                o_ref, r_scr, h_scr, halo_scr, conv_scr, *, tm, rt, blocks_per_seq):
    i = pl.program_id(0)
    f = pl.program_id(1)
    n_blocks = pl.num_programs(0) - 1
    first_tap = CONF_HALO - (CONV_WIDTH - 1)

    def conf_part():
        seq_start = (i % blocks_per_seq) == 0
        h_scr[0:CONF_HALO, :] = jnp.where(seq_start, 0.0, halo_scr[f])
        a = val_ref[0] + bv_ref[0]
        g = gate_ref[0] + bg_ref[0]
        glu = a * _sigmoid(g)
        h_scr[CONF_HALO:CONF_HALO + tm, :] = glu
        zero_glu = _tile_zero(glu)
        halo_scr[f] = h_scr[tm:tm + CONF_HALO, :]
        slot = i % 2
        zeros = []
        for r in range(tm // rt):
            base = r * rt
            acc = dw_ref[0, 0:1, :] * h_scr[base + first_tap:base + first_tap + rt, :]
            for k in range(1, CONV_WIDTH):
                lo = base + first_tap + k
                acc = acc + dw_ref[0, k:k + 1, :] * h_scr[lo:lo + rt, :]
            y = acc + cb_ref[0]
            mu = jnp.mean(y, axis=-1, keepdims=True)
            cen = y - mu
            var = jnp.mean(cen * cen, axis=-1, keepdims=True)
            hn = cen * lax.rsqrt(var + EPS) * (0.5 * lg_ref[0]) + 0.5 * lb_ref[0]
            out = hn + hn * jnp.tanh(hn)
            conv_scr[slot, f, base:base + rt, :] = out.astype(BF16)
            zeros.append(_tile_zero(out))
        return zero_glu, zeros

    def mlp_head():
        @pl.when(f == 0)
        def _():
            slot = (i + 1) % 2
            mix = jnp.concatenate([conv_scr[slot, c] for c in range(N_GROUPS)]
                                  + [dn_ref[c] for c in range(N_GROUPS)], axis=-1)
            h1 = x_ref[...] + jnp.dot(mix, wout_ref[...], preferred_element_type=F32)
            o_ref[...] = h1
            ms = jnp.mean(h1 * h1, axis=-1, keepdims=True)
            r_scr[...] = (h1 * lax.rsqrt(ms + EPS) * n2_ref[...]).astype(BF16)

    def mlp_body(zero_glu=None, zeros=None):
        m = jnp.dot(r_scr[...], wup_ref[...], preferred_element_type=F32)
        n_up = 0
        if zeros is not None:
            up_slab = 2 * LANES
            n_up = m.shape[1] // up_slab
            up_ties = [zero_glu] + zeros[:n_up - 1]
            m = jnp.concatenate(
                [jnp.concatenate([_add_tile(m[:, c * up_slab:c * up_slab + LANES], up_ties[c]),
                                  m[:, c * up_slab + LANES:(c + 1) * up_slab]], axis=1)
                 for c in range(n_up)], axis=1)
        act = jnp.square(jnp.maximum(m, 0.0)).astype(BF16)
        res = jnp.dot(act, wdn_ref[...], preferred_element_type=F32)
        if zeros is None:
            o_ref[...] += res
            return
        slab = o_ref.shape[1] // len(zeros)
        first = n_up - 1
        o_ref[:, 0:first * slab] += res[:, 0:first * slab]
        for n in range(first, len(zeros)):
            c0 = n * slab
            o_ref[:, c0 + LANES:c0 + slab] += res[:, c0 + LANES:c0 + slab]
            o_ref[:, c0:c0 + LANES] += _add_tile(res[:, c0:c0 + LANES], zeros[n])

    def mlp_tail():
        @pl.when(f == pl.num_programs(1) - 1)
        def _():
            h2 = o_ref[...]
            ms = jnp.mean(h2 * h2, axis=-1, keepdims=True)
            o_ref[...] = h2 * lax.rsqrt(ms + EPS) * fn_ref[...]

    @pl.when(i == 0)
    def _():
        @pl.when(f == 0)
        def _():
            halo_scr[...] = jnp.zeros_like(halo_scr)
        conf_part()

    @pl.when(jnp.logical_and(i > 0, i < n_blocks))
    def _():
        mlp_head()
        mlp_body(*conf_part())
        mlp_tail()

    @pl.when(i == n_blocks)
    def _():
        mlp_head()
        mlp_body()
        mlp_tail()


def _mlp(x2, proj3, b_glu, dw_w, dw_b, ln_g, ln_b, dn_o, w_out_b, n2, w_up_b, w_dn_b, fn,
         *, seq, tm=512, rt=64):
    t, d = x2.shape
    dff = w_up_b.shape[1]
    nb = t // tm
    nf = N_GROUPS
    tf = dff // nf
    prev = lambda i: jnp.maximum(i - 1, 0)
    nxt = lambda i: jnp.minimum(i, nb - 1)
    vec = lambda off: pl.BlockSpec((1, 1, LANES), lambda i, f: (off + f, 0, 0))
    return pl.pallas_call(
        functools.partial(_mlp_kernel, tm=tm, rt=rt, blocks_per_seq=seq // tm),
        out_shape=jax.ShapeDtypeStruct((t, d), F32),
        grid=(nb + 1, nf),
        in_specs=[
            pl.BlockSpec((tm, d), lambda i, f: (prev(i), 0)),
            pl.BlockSpec((1, tm, LANES), lambda i, f: (CB_VAL + f, nxt(i), 0)),
            pl.BlockSpec((1, tm, LANES), lambda i, f: (CB_GATE + f, nxt(i), 0)),
            vec(0), vec(N_GROUPS),
            pl.BlockSpec((1, CONV_WIDTH, LANES), lambda i, f: (f, 0, 0)),
            vec(0), vec(0), vec(0),
            pl.BlockSpec((N_GROUPS, tm, LANES), lambda i, f: (0, prev(i), 0)),
            pl.BlockSpec((d, d), lambda i, f: (0, 0), pipeline_mode=pl.Buffered(1)),
            pl.BlockSpec((1, d), lambda i, f: (0, 0)),
            pl.BlockSpec((d, tf), lambda i, f: (0, jnp.where(i == 0, 0, f))),
            pl.BlockSpec((tf, d), lambda i, f: (jnp.where(i == 0, 0, f), 0)),
            pl.BlockSpec((1, d), lambda i, f: (0, 0)),
        ],
        out_specs=pl.BlockSpec((tm, d), lambda i, f: (prev(i), 0)),
        scratch_shapes=[
            pltpu.VMEM((tm, d), BF16),
            pltpu.VMEM((CONF_HALO + tm, LANES), F32),
            pltpu.VMEM((N_GROUPS, CONF_HALO, LANES), F32),
            pltpu.VMEM((2, N_GROUPS, tm, LANES), BF16),
        ],
        compiler_params=pltpu.CompilerParams(
            dimension_semantics=("arbitrary", "arbitrary"),
            vmem_limit_bytes=VMEM_LIMIT),
        name="mlp",
    )(x2, proj3, proj3, b_glu, b_glu, dw_w, dw_b, ln_g, ln_b, dn_o,
      w_out_b, n2, w_up_b, w_dn_b, fn)


def _lane_rep(vals):
    return jnp.broadcast_to(vals.astype(F32)[:, None], (vals.shape[0], LANES))


def kernel(x, norm1_w, w_in, b_glu, conf_dw_w, conf_dw_b, conf_ln_g, conf_ln_b, dn_conv_w,
           dn_a_log, dn_dt_bias, dn_norm_w, w_out, norm2_w, w_mlp_up, w_mlp_down, final_norm_w):
    batch, seq, d = x.shape
    assert norm1_w.shape[0] == 1, "single-layer block"
    h = x.reshape(batch * seq, d)

    w_in_p = _wprep(w_in[0], N_COLBLOCKS * LANES)
    proj3 = _inproj(h, norm1_w[0].reshape(1, d), w_in_p)

    conv_w4 = dn_conv_w[0].reshape(SHORT_CONV_WIDTH, 3, N_GROUPS, LANES)
    dn_o, w_out_b, w_up_b, w_dn_b = _delta(
        proj3, conv_w4, _lane_rep(dn_a_log[0]), _lane_rep(dn_dt_bias[0]),
        dn_norm_w[0].reshape(1, LANES), (w_out[0], w_mlp_up[0], w_mlp_down[0]),
        batch=batch, seq=seq)

    grp = lambda v: v.reshape(-1, 1, LANES)
    dw_w = conf_dw_w[0].reshape(CONV_WIDTH, N_GROUPS, LANES).transpose(1, 0, 2)
    out = _mlp(h, proj3, grp(b_glu[0]), dw_w, grp(conf_dw_b[0]), grp(conf_ln_g[0]),
               grp(conf_ln_b[0]), dn_o, w_out_b, norm2_w[0].reshape(1, d), w_up_b, w_dn_b,
               final_norm_w.reshape(1, d), seq=seq)
    return out.reshape(batch, seq, d)
```

```python
import functools

import jax
import jax.numpy as jnp
from jax import lax
from jax.experimental import pallas as pl
from jax.experimental.pallas import tpu as pltpu

F32 = jnp.float32
BF16 = jnp.bfloat16

EPS = 1e-6
LANES = 128
SUBLANES = 8
CONV_WIDTH = 31
SHORT_CONV_WIDTH = 4
N_GROUPS = 8
CONF_HALO = 32
DN_HALO = SUBLANES
CHUNK = 128
N_LEVELS = CHUNK.bit_length() - 1
HEADS_PER_STEP = 2

CB_VAL, CB_GATE, CB_Q, CB_K, CB_V, CB_Z, CB_BA = 0, 8, 16, 24, 32, 40, 48
N_COLBLOCKS = 50

VMEM_LIMIT = 56 * 1024 * 1024


def _sigmoid(x):
    return 0.5 * jnp.tanh(0.5 * x) + 0.5


def _derived_zero(x):
    bits = lax.bitcast_convert_type(x, jnp.uint32)
    bits = lax.shift_right_logical(lax.shift_right_logical(bits, jnp.uint32(16)), jnp.uint32(16))
    return lax.bitcast_convert_type(bits, F32)


def _tile_zero(x):
    return _derived_zero(jnp.max(x.reshape(-1, SUBLANES, x.shape[-1]), axis=0))


def _add_tile(x, tile):
    n = tile.shape[0]
    return jnp.concatenate([x[0:n] + tile, x[n:]], axis=0)


def _softplus(x):
    return jnp.maximum(x, 0.0) + jnp.log1p(jnp.exp(-jnp.abs(x)))


def _inproj_kernel(x_ref, nw_ref, w_ref, o_ref, u_scr):
    @pl.when(pl.program_id(1) == 0)
    def _():
        x = x_ref[...]
        ms = jnp.mean(x * x, axis=-1, keepdims=True)
        u_scr[...] = (x * lax.rsqrt(ms + EPS) * nw_ref[...]).astype(BF16)

    res = jnp.dot(u_scr[...], w_ref[...], preferred_element_type=F32)
    for c in range(o_ref.shape[0]):
        o_ref[c] = res[:, c * LANES:(c + 1) * LANES]


def _inproj(x2, norm_w, w_in_p, *, tm=1024, tn=1280):
    t, d = x2.shape
    n = w_in_p.shape[1]
    return pl.pallas_call(
        _inproj_kernel,
        out_shape=jax.ShapeDtypeStruct((n // LANES, t, LANES), F32),
        grid=(t // tm, n // tn),
        in_specs=[
            pl.BlockSpec((tm, d), lambda i, j: (i, 0)),
            pl.BlockSpec((1, d), lambda i, j: (0, 0)),
            pl.BlockSpec((d, tn), lambda i, j: (0, j)),
        ],
        out_specs=pl.BlockSpec((tn // LANES, tm, LANES), lambda i, j: (j, i, 0)),
        scratch_shapes=[pltpu.VMEM((tm, d), BF16)],
        compiler_params=pltpu.CompilerParams(
            dimension_semantics=("arbitrary", "arbitrary"),
            vmem_limit_bytes=VMEM_LIMIT),
        name="inproj",
    )(x2, norm_w, w_in_p)


def _delta_kernel(q_ref, k_ref, v_ref, z_ref, ba_ref, cw_ref, alog_ref, dtb_ref, nw_ref,
                  wout_ref, wup_ref, wdn_ref,
                  o_ref, wout_o, wup_o, wdn_o,
                  x_scr, halo_scr, q_scr, k_scr, v_scr,
                  beta_scr, cs_scr, f_scr, ld_scr, mask_scr,
                  u_scr, wq_scr, kgt_scr, intra_scr, ldc_scr, o_scr, s_scr,
                  *, ts, group, rt, n_items, blocks_per_seq):
    step = pl.program_id(0)
    item = jnp.minimum(step, n_items - 1)
    prev = jnp.maximum(step - 1, 0)
    pairs = N_GROUPS // HEADS_PER_STEP
    hp = item % pairs
    i = (item // pairs) % blocks_per_seq
    heads = [HEADS_PER_STEP * hp + hh for hh in range(HEADS_PER_STEP)]
    heads_prev = [HEADS_PER_STEP * (prev % pairs) + hh for hh in range(HEADS_PER_STEP)]
    nc = ts // CHUNK

    wout_o[...] = wout_ref[...].astype(BF16)
    wup_o[...] = wup_ref[...].astype(BF16)
    wdn_o[...] = wdn_ref[...].astype(BF16)

    r_io = lax.broadcasted_iota(jnp.int32, (CHUNK, CHUNK), 0)
    c_io = lax.broadcasted_iota(jnp.int32, (CHUNK, CHUNK), 1)
    causal = r_io >= c_io
    eye = jnp.where(r_io == c_io, 1.0, 0.0).astype(F32)

    @pl.when(hp == 0)
    def _():
        upper = jnp.where(r_io <= c_io, 1.0, 0.0).astype(F32)
        ones = jnp.ones((CHUNK, CHUNK), F32)
        for lvl in range(N_LEVELS):
            same = (r_io >> (lvl + 1)) == (c_io >> (lvl + 1))
            lower_left = jnp.where(((r_io >> lvl) & 1) == 1, 1.0, 0.0) * \
                jnp.where(((c_io >> lvl) & 1) == 0, 1.0, 0.0)
            mask_scr[lvl] = jnp.where(same, lower_left, 0.0).astype(BF16)
        ba_t = ba_ref[0].T
        beta = _sigmoid(ba_t[0:N_GROUPS, :])
        a_raw = ba_t[N_GROUPS:2 * N_GROUPS, :]
        neg_a = -jnp.exp(alog_ref[...])
        for c in range(nc):
            cols = slice(c * CHUNK, (c + 1) * CHUNK)
            g_c = neg_a * _softplus(a_raw[:, cols] + dtb_ref[...])
            cs = jnp.dot(g_c, upper, precision=lax.Precision.HIGHEST,
                         preferred_element_type=F32)
            tot = jnp.dot(g_c, ones, precision=lax.Precision.HIGHEST,
                          preferred_element_type=F32)
            f_c = jnp.exp(tot - cs) * beta[:, cols]
            ld_c = jnp.exp(tot)
            for hh in range(N_GROUPS):
                one = slice(hh, hh + 1)
                beta_scr[hh, :, cols] = beta[one, cols]
                cs_scr[hh, :, cols] = cs[one, :]
                f_scr[hh, :, cols] = f_c[one, :]
                ld_scr[hh, :, cols] = ld_c[one, :]

    @pl.when(i == 0)
    def _():
        for h in heads:
            halo_scr[h] = jnp.zeros((3, DN_HALO, LANES), F32)
            s_scr[h] = jnp.zeros((LANES, LANES), F32)

    set_size = HEADS_PER_STEP * group

    @pl.when(step == 0)
    def _():
        late = slice(set_size, 2 * set_size)
        u_scr[late] = jnp.zeros((set_size,) + u_scr.shape[1:], F32)
        wq_scr[late] = jnp.zeros((set_size,) + wq_scr.shape[1:], BF16)
        kgt_scr[late] = jnp.zeros((set_size,) + kgt_scr.shape[1:], BF16)
        intra_scr[late] = jnp.zeros((set_size,) + intra_scr.shape[1:], BF16)
        ldc_scr[late] = jnp.zeros((set_size,) + ldc_scr.shape[1:], F32)
        o_scr[...] = jnp.zeros_like(o_scr)

    refs = (q_ref, k_ref, v_ref)
    back = SHORT_CONV_WIDTH - 1
    for hh, h in enumerate(heads):
        for j, ref in enumerate(refs):
            x_scr[j, hh, 0:DN_HALO, :] = halo_scr[h, j]
            x_scr[j, hh, DN_HALO:DN_HALO + rt, :] = ref[hh, 0:rt, :]
            halo_scr[h, j] = ref[hh, ts - DN_HALO:ts, :]

    def conv_rows(rows, load):
        tops = []
        for hh, h in enumerate(heads):
            ys = []
            for j in range(3):
                hv = (0.5 * cw_ref[0, j, pl.ds(h, 1), :]) * load(j, hh, 0)
                for m in range(1, SHORT_CONV_WIDTH):
                    hv = hv + (0.5 * cw_ref[m, j, pl.ds(h, 1), :]) * load(j, hh, m)
                ys.append(hv + hv * jnp.tanh(hv))
            qn = ys[0] * (lax.rsqrt(jnp.sum(ys[0] * ys[0], axis=-1, keepdims=True) + EPS)
                          * (LANES ** -0.5))
            kn = ys[1] * lax.rsqrt(jnp.sum(ys[1] * ys[1], axis=-1, keepdims=True) + EPS)
            q_scr[hh, rows, :] = qn
            k_scr[hh, rows, :] = kn
            v_scr[hh, rows, :] = ys[2].astype(BF16)
            tops.append(jnp.maximum(jnp.maximum(qn, kn), ys[2]))
        return _tile_zero(functools.reduce(jnp.maximum, tops))

    conv_rows(pl.ds(0, rt),
              lambda j, hh, m: x_scr[j, hh, DN_HALO - back + m:DN_HALO - back + m + rt, :])

    def conv_tile(r, carry):
        base = pl.multiple_of(r * rt, rt)
        conv_rows(pl.ds(base, rt),
                  lambda j, hh, m: refs[j][hh, pl.ds(base - back + m, rt), :])
        return carry

    n_tiles = ts // rt
    lax.fori_loop(1, n_tiles // 2, conv_tile, 0)

    def conv_late(ties):
        for r in range(n_tiles // 2, n_tiles):
            base = r * rt
            ties.append(conv_rows(
                pl.ds(base, rt),
                lambda j, hh, m: refs[j][hh, base - back + m:base - back + m + rt, :]))
            yield
            yield
            yield

    def prep_group(first_chunk, slot0, ties=()):
        def take_ties():
            while ties:
                it = items[len(ties) % len(items)]
                it["x"] = _add_tile(it["x"], ties.pop())

        items = []
        for hh, h in enumerate(heads):
            for g in range(group):
                slot = slot0 + hh * group + g
                r0 = pl.multiple_of((first_chunk + g) * CHUNK, CHUNK)
                rows = pl.ds(r0, CHUNK)
                k_c = k_scr[hh, rows, :]
                qs_c = q_scr[hh, rows, :]
                cs_r = jnp.broadcast_to(cs_scr[h, :, rows], (CHUNK, CHUNK))
                beta_r = jnp.broadcast_to(beta_scr[h, :, rows], (CHUNK, CHUNK))
                cs_c = cs_r.T
                db = jnp.exp(jnp.where(causal, cs_c - cs_r, -jnp.inf)) * beta_r
                kq = lax.dot_general(
                    jnp.concatenate([k_c, qs_c], axis=0).astype(BF16), k_c.astype(BF16),
                    (((1,), (1,)), ((), ())), preferred_element_type=F32)
                a_mat = kq[:CHUNK] * db
                intra_scr[slot] = (kq[CHUNK:] * db).astype(BF16)
                eg = jnp.exp(cs_c)
                wq_scr[slot, CHUNK:2 * CHUNK, :] = (qs_c * eg).astype(BF16)
                kgt_scr[slot] = (k_c.T * f_scr[h, :, rows]).astype(BF16)
                ldc_scr[slot] = ld_scr[h, :, pl.ds(r0, LANES)]
                rhs = jnp.concatenate([v_scr[hh, rows, :], (k_c * eg).astype(BF16)], axis=1)
                items.append(dict(a=a_mat.astype(BF16), x=eye - a_mat * mask_scr[0].astype(F32),
                                  rhs=rhs, slot=slot))
        yield
        for lvl in range(1, N_LEVELS):
            take_ties()
            for it in items:
                it["xb"] = it["x"].astype(BF16)
                it["mx"] = jnp.dot(it["a"] * mask_scr[lvl], it["xb"], preferred_element_type=F32)
            yield
            for it in items:
                it["x"] = it["x"] - jnp.dot(it["xb"], it["mx"].astype(BF16),
                                            preferred_element_type=F32)
            yield
        take_ties()
        for it in items:
            uw = jnp.dot(it["x"].astype(BF16), it["rhs"], preferred_element_type=F32)
            u_scr[it["slot"]] = uw[:, :LANES]
            wq_scr[it["slot"], 0:CHUNK, :] = uw[:, LANES:].astype(BF16)
        yield

    def step_group(first_chunk, slot0, states, ties=()):
        for g in range(group):
            r0 = pl.multiple_of((first_chunk + g) * CHUNK, CHUNK)
            rows = pl.ds(r0, CHUNK)
            ws = [jnp.dot(wq_scr[slot0 + hh * group + g], states[hh].astype(BF16),
                          preferred_element_type=F32) for hh in range(HEADS_PER_STEP)]
            yield
            for hh in range(HEADS_PER_STEP):
                slot = slot0 + hh * group + g
                v_new = (u_scr[slot] - ws[hh][:CHUNK]).astype(BF16)
                o_c = ws[hh][CHUNK:] + jnp.dot(intra_scr[slot], v_new,
                                               preferred_element_type=F32)
                if ties:
                    o_c = _add_tile(o_c, ties.pop())
                o_scr[hh, rows, :] = o_c
                states[hh] = states[hh] * ldc_scr[slot] + jnp.dot(
                    kgt_scr[slot], v_new, preferred_element_type=F32)
            yield

    def interleave(*gens):
        live = list(gens)
        while live:
            for gen in list(live):
                try:
                    next(gen)
                except StopIteration:
                    live.remove(gen)

    assert nc == 2 * group and n_tiles % 2 == 0

    def out_rows(rows):
        tops = []
        for hh in range(HEADS_PER_STEP):
            o = o_scr[hh, rows, :]
            o = o * lax.rsqrt(jnp.mean(o * o, axis=-1, keepdims=True) + EPS) * nw_ref[...]
            hz = 0.5 * z_ref[hh, rows, :]
            res = o * (hz + hz * jnp.tanh(hz))
            o_ref[hh, rows, :] = res.astype(o_ref.dtype)
            tops.append(res)
        return _tile_zero(functools.reduce(jnp.maximum, tops))

    def out_tiles(ties):
        for r in range(n_tiles):
            ties.append(out_rows(pl.ds(r * rt, rt)))
            for _ in range(2 * group // n_tiles):
                yield

    conv_ties = []
    states = [s_scr[h] for h in heads_prev]
    interleave(conv_late(conv_ties), prep_group(0, 0, conv_ties),
               step_group(group, set_size, states))
    for hh, h in enumerate(heads_prev):
        s_scr[h] = states[hh]

    out_ties = []
    states = [s_scr[h] for h in heads]
    interleave(out_tiles(out_ties), step_group(0, 0, states, out_ties),
               prep_group(group, set_size))
    for hh, h in enumerate(heads):
        s_scr[h] = states[hh]


def _delta(proj3, conv_w4, alog_rep, dtb_rep, norm_row, weights, *, batch, seq,
           ts=2048, group=8, rt=256):
    t = proj3.shape[1]
    ns = seq // ts
    hps = HEADS_PER_STEP
    pairs = N_GROUPS // hps
    n_items = batch * ns * pairs
    for w in weights:
        assert w.shape[0] % n_items == 0 and (w.shape[0] // n_items) % 16 == 0, w.shape
    w_spec = lambda w: pl.BlockSpec((w.shape[0] // n_items, w.shape[1]),
                                    lambda s: (jnp.minimum(s, n_items - 1), 0))

    def cur(off):
        def index_map(s):
            it = jnp.minimum(s, n_items - 1)
            return (off // hps + it % pairs, it // pairs, 0)
        return index_map

    def prv(off):
        def index_map(s):
            it = jnp.maximum(s - 1, 0)
            return (off // hps + it % pairs, it // pairs, 0)
        return index_map

    blk = lambda index_map: pl.BlockSpec((hps, ts, LANES), index_map)
    rep = pl.BlockSpec((N_GROUPS, LANES), lambda s: (0, 0))
    n_slots = 2 * hps * group
    return pl.pallas_call(
        functools.partial(_delta_kernel, ts=ts, group=group, rt=rt, n_items=n_items,
                          blocks_per_seq=ns),
        out_shape=(jax.ShapeDtypeStruct((N_GROUPS, t, LANES), BF16),)
        + tuple(jax.ShapeDtypeStruct(w.shape, BF16) for w in weights),
        grid=(n_items + 1,),
        in_specs=[
            blk(cur(CB_Q)), blk(cur(CB_K)), blk(cur(CB_V)), blk(prv(CB_Z)),
            pl.BlockSpec((1, ts, LANES),
                         lambda s: (CB_BA, jnp.minimum(s, n_items - 1) // pairs, 0)),
            pl.BlockSpec((SHORT_CONV_WIDTH, 3, N_GROUPS, LANES), lambda s: (0, 0, 0, 0)),
            rep, rep,
            pl.BlockSpec((1, LANES), lambda s: (0, 0)),
        ] + [w_spec(w) for w in weights],
        out_specs=(blk(prv(0)),) + tuple(w_spec(w) for w in weights),
        scratch_shapes=[
            pltpu.VMEM((3, hps, DN_HALO + rt, LANES), F32),
            pltpu.VMEM((N_GROUPS, 3, DN_HALO, LANES), F32),
            pltpu.VMEM((hps, ts, LANES), F32),
            pltpu.VMEM((hps, ts, LANES), F32),
            pltpu.VMEM((hps, ts, LANES), BF16),
            pltpu.VMEM((N_GROUPS, 1, ts), F32),
            pltpu.VMEM((N_GROUPS, 1, ts), F32),
            pltpu.VMEM((N_GROUPS, 1, ts), F32),
            pltpu.VMEM((N_GROUPS, 1, ts), F32),
            pltpu.VMEM((N_LEVELS, CHUNK, CHUNK), BF16),
            pltpu.VMEM((n_slots, CHUNK, LANES), F32),
            pltpu.VMEM((n_slots, 2 * CHUNK, LANES), BF16),
            pltpu.VMEM((n_slots, LANES, CHUNK), BF16),
            pltpu.VMEM((n_slots, CHUNK, CHUNK), BF16),
            pltpu.VMEM((n_slots, 1, LANES), F32),
            pltpu.VMEM((hps, ts, LANES), F32),
            pltpu.VMEM((N_GROUPS, LANES, LANES), F32),
        ],
        compiler_params=pltpu.CompilerParams(
            dimension_semantics=("arbitrary",),
            vmem_limit_bytes=VMEM_LIMIT),
        name="delta",
    )(proj3, proj3, proj3, proj3, proj3, conv_w4, alog_rep, dtb_rep, norm_row, *weights)


def _mlp_kernel(x_ref, val_ref, gate_ref, bv_ref, bg_ref, dw_ref, cb_ref, lg_ref, lb_ref,
                dn_ref, wout_ref, n2_ref, wup_ref, wdn_ref, fn_ref,
                o_ref, r_scr, h_scr, halo_scr, conv_scr, *, tm, rt, blocks_per_seq):
    i = pl.program_id(0)
    f = pl.program_id(1)
    n_blocks = pl.num_programs(0) - 1
    first_tap = CONF_HALO - (CONV_WIDTH - 1)

    def conf_part():
        seq_start = (i % blocks_per_seq) == 0
        h_scr[0:CONF_HALO, :] = jnp.where(seq_start, 0.0, halo_scr[f])
        a = val_ref[0] + bv_ref[0]
        g = gate_ref[0] + bg_ref[0]
        glu = a * _sigmoid(g)
        h_scr[CONF_HALO:CONF_HALO + tm, :] = glu
        zero_glu = _tile_zero(glu)
        halo_scr[f] = h_scr[tm:tm + CONF_HALO, :]
        slot = i % 2
        zeros = []
        for r in range(tm // rt):
            base = r * rt
            acc = dw_ref[0, 0:1, :] * h_scr[base + first_tap:base + first_tap + rt, :]
            for k in range(1, CONV_WIDTH):
                lo = base + first_tap + k
                acc = acc + dw_ref[0, k:k + 1, :] * h_scr[lo:lo + rt, :]
            y = acc + cb_ref[0]
            mu = jnp.mean(y, axis=-1, keepdims=True)
            cen = y - mu
            var = jnp.mean(cen * cen, axis=-1, keepdims=True)
            hn = cen * lax.rsqrt(var + EPS) * (0.5 * lg_ref[0]) + 0.5 * lb_ref[0]
            out = hn + hn * jnp.tanh(hn)
            conv_scr[slot, f, base:base + rt, :] = out.astype(BF16)
            zeros.append(_tile_zero(out))
        return zero_glu, zeros

    def mlp_head():
        @pl.when(f == 0)
        def _():
            slot = (i + 1) % 2
            mix = jnp.concatenate([conv_scr[slot, c] for c in range(N_GROUPS)]
                                  + [dn_ref[c] for c in range(N_GROUPS)], axis=-1)
            h1 = x_ref[...] + jnp.dot(mix, wout_ref[...], preferred_element_type=F32)
            o_ref[...] = h1
            ms = jnp.mean(h1 * h1, axis=-1, keepdims=True)
            r_scr[...] = (h1 * lax.rsqrt(ms + EPS) * n2_ref[...]).astype(BF16)

    def mlp_body(zero_glu=None, zeros=None):
        m = jnp.dot(r_scr[...], wup_ref[...], preferred_element_type=F32)
        n_up = 0
        if zeros is not None:
            up_slab = 2 * LANES
            n_up = m.shape[1] // up_slab
            up_ties = [zero_glu] + zeros[:n_up - 1]
            m = jnp.concatenate(
                [jnp.concatenate([_add_tile(m[:, c * up_slab:c * up_slab + LANES], up_ties[c]),
                                  m[:, c * up_slab + LANES:(c + 1) * up_slab]], axis=1)
                 for c in range(n_up)], axis=1)
        act = jnp.square(jnp.maximum(m, 0.0)).astype(BF16)
        res = jnp.dot(act, wdn_ref[...], preferred_element_type=F32)
        if zeros is None:
            o_ref[...] += res
            return
        slab = o_ref.shape[1] // len(zeros)
        first = n_up - 1
        o_ref[:, 0:first * slab] += res[:, 0:first * slab]
        for n in range(first, len(zeros)):
            c0 = n * slab
            o_ref[:, c0 + LANES:c0 + slab] += res[:, c0 + LANES:c0 + slab]
            o_ref[:, c0:c0 + LANES] += _add_tile(res[:, c0:c0 + LANES], zeros[n])

    def mlp_tail():
        @pl.when(f == pl.num_programs(1) - 1)
        def _():
            h2 = o_ref[...]
            ms = jnp.mean(h2 * h2, axis=-1, keepdims=True)
            o_ref[...] = h2 * lax.rsqrt(ms + EPS) * fn_ref[...]

    @pl.when(i == 0)
    def _():
        @pl.when(f == 0)
        def _():
            halo_scr[...] = jnp.zeros_like(halo_scr)
        conf_part()

    @pl.when(jnp.logical_and(i > 0, i < n_blocks))
    def _():
        mlp_head()
        mlp_body(*conf_part())
        mlp_tail()

    @pl.when(i == n_blocks)
    def _():
        mlp_head()
        mlp_body()
        mlp_tail()


def _mlp(x2, proj3, b_glu, dw_w, dw_b, ln_g, ln_b, dn_o, w_out_b, n2, w_up_b, w_dn_b, fn,
         *, seq, tm=512, rt=64):
    t, d = x2.shape
    dff = w_up_b.shape[1]
    nb = t // tm
    nf = N_GROUPS
    tf = dff // nf
    prev = lambda i: jnp.maximum(i - 1, 0)
    nxt = lambda i: jnp.minimum(i, nb - 1)
    vec = lambda off: pl.BlockSpec((1, 1, LANES), lambda i, f: (off + f, 0, 0))
    return pl.pallas_call(
        functools.partial(_mlp_kernel, tm=tm, rt=rt, blocks_per_seq=seq // tm),
        out_shape=jax.ShapeDtypeStruct((t, d), F32),
        grid=(nb + 1, nf),
        in_specs=[
            pl.BlockSpec((tm, d), lambda i, f: (prev(i), 0)),
            pl.BlockSpec((1, tm, LANES), lambda i, f: (CB_VAL + f, nxt(i), 0)),
            pl.BlockSpec((1, tm, LANES), lambda i, f: (CB_GATE + f, nxt(i), 0)),
            vec(0), vec(N_GROUPS),
            pl.BlockSpec((1, CONV_WIDTH, LANES), lambda i, f: (f, 0, 0)),
            vec(0), vec(0), vec(0),
            pl.BlockSpec((N_GROUPS, tm, LANES), lambda i, f: (0, prev(i), 0)),
            pl.BlockSpec((d, d), lambda i, f: (0, 0), pipeline_mode=pl.Buffered(1)),
            pl.BlockSpec((1, d), lambda i, f: (0, 0)),
            pl.BlockSpec((d, tf), lambda i, f: (0, jnp.where(i == 0, 0, f))),
            pl.BlockSpec((tf, d), lambda i, f: (jnp.where(i == 0, 0, f), 0)),
            pl.BlockSpec((1, d), lambda i, f: (0, 0)),
        ],
        out_specs=pl.BlockSpec((tm, d), lambda i, f: (prev(i), 0)),
        scratch_shapes=[
            pltpu.VMEM((tm, d), BF16),
            pltpu.VMEM((CONF_HALO + tm, LANES), F32),
            pltpu.VMEM((N_GROUPS, CONF_HALO, LANES), F32),
            pltpu.VMEM((2, N_GROUPS, tm, LANES), BF16),
        ],
        compiler_params=pltpu.CompilerParams(
            dimension_semantics=("arbitrary", "arbitrary"),
            vmem_limit_bytes=VMEM_LIMIT),
        name="mlp",
    )(x2, proj3, proj3, b_glu, b_glu, dw_w, dw_b, ln_g, ln_b, dn_o,
      w_out_b, n2, w_up_b, w_dn_b, fn)


def _lane_rep(vals):
    return jnp.broadcast_to(vals.astype(F32)[:, None], (vals.shape[0], LANES))


def kernel(x, norm1_w, w_in, b_glu, conf_dw_w, conf_dw_b, conf_ln_g, conf_ln_b, dn_conv_w,
           dn_a_log, dn_dt_bias, dn_norm_w, w_out, norm2_w, w_mlp_up, w_mlp_down, final_norm_w):
    batch, seq, d = x.shape
    assert norm1_w.shape[0] == 1, "single-layer block"
    h = x.reshape(batch * seq, d)

    d_in = w_in.shape[2]
    w_in_p = jnp.pad(w_in[0].astype(BF16), ((0, 0), (0, N_COLBLOCKS * LANES - d_in)))
    proj3 = _inproj(h, norm1_w[0].reshape(1, d), w_in_p)

    conv_w4 = dn_conv_w[0].reshape(SHORT_CONV_WIDTH, 3, N_GROUPS, LANES)
    dn_o, w_out_b, w_up_b, w_dn_b = _delta(
        proj3, conv_w4, _lane_rep(dn_a_log[0]), _lane_rep(dn_dt_bias[0]),
        dn_norm_w[0].reshape(1, LANES), (w_out[0], w_mlp_up[0], w_mlp_down[0]),
        batch=batch, seq=seq)

    grp = lambda v: v.reshape(-1, 1, LANES)
    dw_w = conf_dw_w[0].reshape(CONV_WIDTH, N_GROUPS, LANES).transpose(1, 0, 2)
    out = _mlp(h, proj3, grp(b_glu[0]), dw_w, grp(conf_dw_b[0]), grp(conf_ln_g[0]),
               grp(conf_ln_b[0]), dn_o, w_out_b, norm2_w[0].reshape(1, d), w_up_b, w_dn_b,
               final_norm_w.reshape(1, d), seq=seq)
    return out.reshape(batch, seq, d)
```

```python
import functools

import jax
import jax.numpy as jnp
from jax import lax
from jax.experimental import pallas as pl
from jax.experimental.pallas import tpu as pltpu

F32 = jnp.float32
BF16 = jnp.bfloat16

EPS = 1e-6
LANES = 128
SUBLANES = 8
CONV_WIDTH = 31
SHORT_CONV_WIDTH = 4
N_GROUPS = 8
CONF_HALO = 32
DN_HALO = SUBLANES
CHUNK = 128
N_LEVELS = CHUNK.bit_length() - 1
HEADS_PER_STEP = 2

CB_VAL, CB_GATE, CB_Q, CB_K, CB_V, CB_Z, CB_BA = 0, 8, 16, 24, 32, 40, 48
N_COLBLOCKS = 50

VMEM_LIMIT = 56 * 1024 * 1024


def _sigmoid(x):
    return 0.5 * jnp.tanh(0.5 * x) + 0.5


def _derived_zero(x):
    bits = lax.bitcast_convert_type(x, jnp.uint32)
    bits = lax.shift_right_logical(lax.shift_right_logical(bits, jnp.uint32(16)), jnp.uint32(16))
    return lax.bitcast_convert_type(bits, F32)


def _tile_zero(x):
    return _derived_zero(jnp.max(x.reshape(-1, SUBLANES, x.shape[-1]), axis=0))


def _add_tile(x, tile):
    n = tile.shape[0]
    return jnp.concatenate([x[0:n] + tile, x[n:]], axis=0)


def _softplus(x):
    return jnp.maximum(x, 0.0) + jnp.log1p(jnp.exp(-jnp.abs(x)))


def _inproj_kernel(x_ref, nw_ref, w_ref, o_ref, u_scr):
    @pl.when(pl.program_id(1) == 0)
    def _():
        x = x_ref[...]
        ms = jnp.mean(x * x, axis=-1, keepdims=True)
        u_scr[...] = (x * lax.rsqrt(ms + EPS) * nw_ref[...]).astype(BF16)

    res = jnp.dot(u_scr[...], w_ref[...], preferred_element_type=F32)
    for c in range(o_ref.shape[0]):
        o_ref[c] = res[:, c * LANES:(c + 1) * LANES]


def _inproj(x2, norm_w, w_in_p, *, tm=1024, tn=1280):
    t, d = x2.shape
    n = w_in_p.shape[1]
    return pl.pallas_call(
        _inproj_kernel,
        out_shape=jax.ShapeDtypeStruct((n // LANES, t, LANES), F32),
        grid=(t // tm, n // tn),
        in_specs=[
            pl.BlockSpec((tm, d), lambda i, j: (i, 0)),
            pl.BlockSpec((1, d), lambda i, j: (0, 0)),
            pl.BlockSpec((d, tn), lambda i, j: (0, j)),
        ],
        out_specs=pl.BlockSpec((tn // LANES, tm, LANES), lambda i, j: (j, i, 0)),
        scratch_shapes=[pltpu.VMEM((tm, d), BF16)],
        compiler_params=pltpu.CompilerParams(
            dimension_semantics=("arbitrary", "arbitrary"),
            vmem_limit_bytes=VMEM_LIMIT),
        name="inproj",
    )(x2, norm_w, w_in_p)


def _delta_kernel(q_ref, k_ref, v_ref, z_ref, ba_ref, cw_ref, alog_ref, dtb_ref, nw_ref,
                  wout_ref, wup_ref, wdn_ref,
                  o_ref, wout_o, wup_o, wdn_o,
                  x_scr, halo_scr, q_scr, k_scr, v_scr,
                  beta_scr, cs_scr, f_scr, ld_scr, mask_scr,
                  u_scr, wq_scr, kgt_scr, intra_scr, ldc_scr, o_scr, s_scr,
                  *, ts, group, rt, n_items, blocks_per_seq):
    step = pl.program_id(0)
    item = jnp.minimum(step, n_items - 1)
    prev = jnp.maximum(step - 1, 0)
    pairs = N_GROUPS // HEADS_PER_STEP
    hp = item % pairs
    i = (item // pairs) % blocks_per_seq
    heads = [HEADS_PER_STEP * hp + hh for hh in range(HEADS_PER_STEP)]
    heads_prev = [HEADS_PER_STEP * (prev % pairs) + hh for hh in range(HEADS_PER_STEP)]
    nc = ts // CHUNK

    wout_o[...] = wout_ref[...].astype(BF16)
    wup_o[...] = wup_ref[...].astype(BF16)
    wdn_o[...] = wdn_ref[...].astype(BF16)

    r_io = lax.broadcasted_iota(jnp.int32, (CHUNK, CHUNK), 0)
    c_io = lax.broadcasted_iota(jnp.int32, (CHUNK, CHUNK), 1)
    causal = r_io >= c_io
    eye = jnp.where(r_io == c_io, 1.0, 0.0).astype(F32)

    @pl.when(hp == 0)
    def _():
        upper = jnp.where(r_io <= c_io, 1.0, 0.0).astype(F32)
        ones = jnp.ones((CHUNK, CHUNK), F32)
        for lvl in range(N_LEVELS):
            same = (r_io >> (lvl + 1)) == (c_io >> (lvl + 1))
            lower_left = jnp.where(((r_io >> lvl) & 1) == 1, 1.0, 0.0) * \
                jnp.where(((c_io >> lvl) & 1) == 0, 1.0, 0.0)
            mask_scr[lvl] = jnp.where(same, lower_left, 0.0).astype(BF16)
        ba_t = ba_ref[0].T
        beta = _sigmoid(ba_t[0:N_GROUPS, :])
        a_raw = ba_t[N_GROUPS:2 * N_GROUPS, :]
        neg_a = -jnp.exp(alog_ref[...])
        for c in range(nc):
            cols = slice(c * CHUNK, (c + 1) * CHUNK)
            g_c = neg_a * _softplus(a_raw[:, cols] + dtb_ref[...])
            cs = jnp.dot(g_c, upper, precision=lax.Precision.HIGHEST,
                         preferred_element_type=F32)
            tot = jnp.dot(g_c, ones, precision=lax.Precision.HIGHEST,
                          preferred_element_type=F32)
            f_c = jnp.exp(tot - cs) * beta[:, cols]
            ld_c = jnp.exp(tot)
            for hh in range(N_GROUPS):
                one = slice(hh, hh + 1)
                beta_scr[hh, :, cols] = beta[one, cols]
                cs_scr[hh, :, cols] = cs[one, :]
                f_scr[hh, :, cols] = f_c[one, :]
                ld_scr[hh, :, cols] = ld_c[one, :]

    @pl.when(i == 0)
    def _():
        for h in heads:
            halo_scr[h] = jnp.zeros((3, DN_HALO, LANES), F32)
            s_scr[h] = jnp.zeros((LANES, LANES), F32)

    set_size = HEADS_PER_STEP * group

    @pl.when(step == 0)
    def _():
        late = slice(set_size, 2 * set_size)
        u_scr[late] = jnp.zeros((set_size,) + u_scr.shape[1:], F32)
        wq_scr[late] = jnp.zeros((set_size,) + wq_scr.shape[1:], BF16)
        kgt_scr[late] = jnp.zeros((set_size,) + kgt_scr.shape[1:], BF16)
        intra_scr[late] = jnp.zeros((set_size,) + intra_scr.shape[1:], BF16)
        ldc_scr[late] = jnp.zeros((set_size,) + ldc_scr.shape[1:], F32)
        o_scr[...] = jnp.zeros_like(o_scr)

    refs = (q_ref, k_ref, v_ref)
    back = SHORT_CONV_WIDTH - 1
    for hh, h in enumerate(heads):
        for j, ref in enumerate(refs):
            x_scr[j, hh, 0:DN_HALO, :] = halo_scr[h, j]
            x_scr[j, hh, DN_HALO:DN_HALO + rt, :] = ref[hh, 0:rt, :]
            halo_scr[h, j] = ref[hh, ts - DN_HALO:ts, :]

    def conv_rows(rows, load):
        tops = []
        for hh, h in enumerate(heads):
            ys = []
            for j in range(3):
                hv = (0.5 * cw_ref[0, j, pl.ds(h, 1), :]) * load(j, hh, 0)
                for m in range(1, SHORT_CONV_WIDTH):
                    hv = hv + (0.5 * cw_ref[m, j, pl.ds(h, 1), :]) * load(j, hh, m)
                ys.append(hv + hv * jnp.tanh(hv))
            qn = ys[0] * (lax.rsqrt(jnp.sum(ys[0] * ys[0], axis=-1, keepdims=True) + EPS)
                          * (LANES ** -0.5))
            kn = ys[1] * lax.rsqrt(jnp.sum(ys[1] * ys[1], axis=-1, keepdims=True) + EPS)
            q_scr[hh, rows, :] = qn
            k_scr[hh, rows, :] = kn
            v_scr[hh, rows, :] = ys[2].astype(BF16)
            tops.append(jnp.maximum(jnp.maximum(qn, kn), ys[2]))
        return _tile_zero(functools.reduce(jnp.maximum, tops))

    conv_rows(pl.ds(0, rt),
              lambda j, hh, m: x_scr[j, hh, DN_HALO - back + m:DN_HALO - back + m + rt, :])

    def conv_tile(r, carry):
        base = pl.multiple_of(r * rt, rt)
        conv_rows(pl.ds(base, rt),
                  lambda j, hh, m: refs[j][hh, pl.ds(base - back + m, rt), :])
        return carry

    n_tiles = ts // rt
    lax.fori_loop(1, n_tiles // 2, conv_tile, 0)

    def conv_late(ties):
        for r in range(n_tiles // 2, n_tiles):
            base = r * rt
            ties.append(conv_rows(
                pl.ds(base, rt),
                lambda j, hh, m: refs[j][hh, base - back + m:base - back + m + rt, :]))
            yield
            yield
            yield

    def prep_group(first_chunk, slot0, ties=()):
        def take_ties():
            while ties:
                it = items[len(ties) % len(items)]
                it["x"] = _add_tile(it["x"], ties.pop())

        items = []
        for hh, h in enumerate(heads):
            for g in range(group):
                slot = slot0 + hh * group + g
                r0 = pl.multiple_of((first_chunk + g) * CHUNK, CHUNK)
                rows = pl.ds(r0, CHUNK)
                k_c = k_scr[hh, rows, :]
                qs_c = q_scr[hh, rows, :]
                cs_r = jnp.broadcast_to(cs_scr[h, :, rows], (CHUNK, CHUNK))
                beta_r = jnp.broadcast_to(beta_scr[h, :, rows], (CHUNK, CHUNK))
                cs_c = cs_r.T
                db = jnp.exp(jnp.where(causal, cs_c - cs_r, -jnp.inf)) * beta_r
                kq = lax.dot_general(
                    jnp.concatenate([k_c, qs_c], axis=0).astype(BF16), k_c.astype(BF16),
                    (((1,), (1,)), ((), ())), preferred_element_type=F32)
                a_mat = kq[:CHUNK] * db
                intra_scr[slot] = (kq[CHUNK:] * db).astype(BF16)
                eg = jnp.exp(cs_c)
                wq_scr[slot, CHUNK:2 * CHUNK, :] = (qs_c * eg).astype(BF16)
                kgt_scr[slot] = (k_c.T * f_scr[h, :, rows]).astype(BF16)
                ldc_scr[slot] = ld_scr[h, :, pl.ds(r0, LANES)]
                rhs = jnp.concatenate([v_scr[hh, rows, :], (k_c * eg).astype(BF16)], axis=1)
                items.append(dict(a=a_mat.astype(BF16), x=eye - a_mat * mask_scr[0].astype(F32),
                                  rhs=rhs, slot=slot))
        yield
        for lvl in range(1, N_LEVELS):
            take_ties()
            for it in items:
                it["xb"] = it["x"].astype(BF16)
                it["mx"] = jnp.dot(it["a"] * mask_scr[lvl], it["xb"], preferred_element_type=F32)
            yield
            for it in items:
                it["x"] = it["x"] - jnp.dot(it["xb"], it["mx"].astype(BF16),
                                            preferred_element_type=F32)
            yield
        take_ties()
        for it in items:
            uw = jnp.dot(it["x"].astype(BF16), it["rhs"], preferred_element_type=F32)
            u_scr[it["slot"]] = uw[:, :LANES]
            wq_scr[it["slot"], 0:CHUNK, :] = uw[:, LANES:].astype(BF16)
        yield

    def step_group(first_chunk, slot0, states, ties=()):
        for g in range(group):
            r0 = pl.multiple_of((first_chunk + g) * CHUNK, CHUNK)
            rows = pl.ds(r0, CHUNK)
            ws = [jnp.dot(wq_scr[slot0 + hh * group + g], states[hh].astype(BF16),
                          preferred_element_type=F32) for hh in range(HEADS_PER_STEP)]
            yield
            for hh in range(HEADS_PER_STEP):
                slot = slot0 + hh * group + g
                v_new = (u_scr[slot] - ws[hh][:CHUNK]).astype(BF16)
                o_c = ws[hh][CHUNK:] + jnp.dot(intra_scr[slot], v_new,
                                               preferred_element_type=F32)
                if ties:
                    o_c = _add_tile(o_c, ties.pop())
                o_scr[hh, rows, :] = o_c
                states[hh] = states[hh] * ldc_scr[slot] + jnp.dot(
                    kgt_scr[slot], v_new, preferred_element_type=F32)
            yield

    def interleave(*gens):
        live = list(gens)
        while live:
            for gen in list(live):
                try:
                    next(gen)
                except StopIteration:
                    live.remove(gen)

    assert nc == 2 * group and n_tiles % 2 == 0

    def out_rows(rows):
        tops = []
        for hh in range(HEADS_PER_STEP):
            o = o_scr[hh, rows, :]
            o = o * lax.rsqrt(jnp.mean(o * o, axis=-1, keepdims=True) + EPS) * nw_ref[...]
            hz = 0.5 * z_ref[hh, rows, :]
            res = o * (hz + hz * jnp.tanh(hz))
            o_ref[hh, rows, :] = res.astype(o_ref.dtype)
            tops.append(res)
        return _tile_zero(functools.reduce(jnp.maximum, tops))

    def out_tiles(ties):
        for r in range(n_tiles):
            ties.append(out_rows(pl.ds(r * rt, rt)))
            for _ in range(2 * group // n_tiles):
                yield

    conv_ties = []
    states = [s_scr[h] for h in heads_prev]
    interleave(conv_late(conv_ties), prep_group(0, 0, conv_ties),
               step_group(group, set_size, states))
    for hh, h in enumerate(heads_prev):
        s_scr[h] = states[hh]

    out_ties = []
    states = [s_scr[h] for h in heads]
    interleave(out_tiles(out_ties), step_group(0, 0, states, out_ties),
               prep_group(group, set_size))
    for hh, h in enumerate(heads):
        s_scr[h] = states[hh]


def _delta(proj3, conv_w4, alog_rep, dtb_rep, norm_row, weights, *, batch, seq,
           ts=2048, group=8, rt=256):
    t = proj3.shape[1]
    ns = seq // ts
    hps = HEADS_PER_STEP
    pairs = N_GROUPS // hps
    n_items = batch * ns * pairs
    for w in weights:
        assert w.shape[0] % n_items == 0 and (w.shape[0] // n_items) % 16 == 0, w.shape
    w_spec = lambda w: pl.BlockSpec((w.shape[0] // n_items, w.shape[1]),
                                    lambda s: (jnp.minimum(s, n_items - 1), 0))

    def cur(off):
        def index_map(s):
            it = jnp.minimum(s, n_items - 1)
            return (off // hps + it % pairs, it // pairs, 0)
        return index_map

    def prv(off):
        def index_map(s):
            it = jnp.maximum(s - 1, 0)
            return (off // hps + it % pairs, it // pairs, 0)
        return index_map

    blk = lambda index_map: pl.BlockSpec((hps, ts, LANES), index_map)
    rep = pl.BlockSpec((N_GROUPS, LANES), lambda s: (0, 0))
    n_slots = 2 * hps * group
    return pl.pallas_call(
        functools.partial(_delta_kernel, ts=ts, group=group, rt=rt, n_items=n_items,
                          blocks_per_seq=ns),
        out_shape=(jax.ShapeDtypeStruct((N_GROUPS, t, LANES), BF16),)
        + tuple(jax.ShapeDtypeStruct(w.shape, BF16) for w in weights),
        grid=(n_items + 1,),
        in_specs=[
            blk(cur(CB_Q)), blk(cur(CB_K)), blk(cur(CB_V)), blk(prv(CB_Z)),
            pl.BlockSpec((1, ts, LANES),
                         lambda s: (CB_BA, jnp.minimum(s, n_items - 1) // pairs, 0)),
            pl.BlockSpec((SHORT_CONV_WIDTH, 3, N_GROUPS, LANES), lambda s: (0, 0, 0, 0)),
            rep, rep,
            pl.BlockSpec((1, LANES), lambda s: (0, 0)),
        ] + [w_spec(w) for w in weights],
        out_specs=(blk(prv(0)),) + tuple(w_spec(w) for w in weights),
        scratch_shapes=[
            pltpu.VMEM((3, hps, DN_HALO + rt, LANES), F32),
            pltpu.VMEM((N_GROUPS, 3, DN_HALO, LANES), F32),
            pltpu.VMEM((hps, ts, LANES), F32),
            pltpu.VMEM((hps, ts, LANES), F32),
            pltpu.VMEM((hps, ts, LANES), BF16),
            pltpu.VMEM((N_GROUPS, 1, ts), F32),
            pltpu.VMEM((N_GROUPS, 1, ts), F32),
            pltpu.VMEM((N_GROUPS, 1, ts), F32),
            pltpu.VMEM((N_GROUPS, 1, ts), F32),
            pltpu.VMEM((N_LEVELS, CHUNK, CHUNK), BF16),
            pltpu.VMEM((n_slots, CHUNK, LANES), F32),
            pltpu.VMEM((n_slots, 2 * CHUNK, LANES), BF16),
            pltpu.VMEM((n_slots, LANES, CHUNK), BF16),
            pltpu.VMEM((n_slots, CHUNK, CHUNK), BF16),
            pltpu.VMEM((n_slots, 1, LANES), F32),
            pltpu.VMEM((hps, ts, LANES), F32),
            pltpu.VMEM((N_GROUPS, LANES, LANES), F32),
        ],
        compiler_params=pltpu.CompilerParams(
            dimension_semantics=("arbitrary",),
            vmem_limit_bytes=VMEM_LIMIT),
        name="delta",
    )(proj3, proj3, proj3, proj3, proj3, conv_w4, alog_rep, dtb_rep, norm_row, *weights)


def _mlp_kernel(x_ref, val_ref, gate_ref, bv_ref, bg_ref, dw_ref, cb_ref, lg_ref, lb_ref,
                dn_ref, wout_ref, n2_ref, wup_ref, wdn_ref, fn_ref,
                o_ref, r_scr, h_scr, halo_scr, conv_scr, *, tm, rt, blocks_per_seq):
    i = pl.program_id(0)
    f = pl.program_id(1)
    n_blocks = pl.num_programs(0) - 1
    first_tap = CONF_HALO - (CONV_WIDTH - 1)

    def conf_part():
        seq_start = (i % blocks_per_seq) == 0
        h_scr[0:CONF_HALO, :] = jnp.where(seq_start, 0.0, halo_scr[f])
        a = val_ref[0] + bv_ref[0]
        g = gate_ref[0] + bg_ref[0]
        glu = a * _sigmoid(g)
        h_scr[CONF_HALO:CONF_HALO + tm, :] = glu
        zero_glu = _tile_zero(glu)
        halo_scr[f] = h_scr[tm:tm + CONF_HALO, :]
        slot = i % 2
        zeros = []
        for r in range(tm // rt):
            base = r * rt
            acc = dw_ref[0, 0:1, :] * h_scr[base + first_tap:base + first_tap + rt, :]
            for k in range(1, CONV_WIDTH):
                lo = base + first_tap + k
                acc = acc + dw_ref[0, k:k + 1, :] * h_scr[lo:lo + rt, :]
            y = acc + cb_ref[0]
            mu = jnp.mean(y, axis=-1, keepdims=True)
            cen = y - mu
            var = jnp.mean(cen * cen, axis=-1, keepdims=True)
            hn = cen * lax.rsqrt(var + EPS) * (0.5 * lg_ref[0]) + 0.5 * lb_ref[0]
            out = hn + hn * jnp.tanh(hn)
            conv_scr[slot, f, base:base + rt, :] = out.astype(BF16)
            zeros.append(_tile_zero(out))
        return zero_glu, zeros

    def mlp_head():
        @pl.when(f == 0)
        def _():
            slot = (i + 1) % 2
            mix = jnp.concatenate([conv_scr[slot, c] for c in range(N_GROUPS)]
                                  + [dn_ref[c] for c in range(N_GROUPS)], axis=-1)
            h1 = x_ref[...] + jnp.dot(mix, wout_ref[...], preferred_element_type=F32)
            o_ref[...] = h1
            ms = jnp.mean(h1 * h1, axis=-1, keepdims=True)
            r_scr[...] = (h1 * lax.rsqrt(ms + EPS) * n2_ref[...]).astype(BF16)

    def mlp_body(zero_glu=None, zeros=None):
        m = jnp.dot(r_scr[...], wup_ref[...], preferred_element_type=F32)
        n_up = 0
        if zeros is not None:
            up_slab = 2 * LANES
            n_up = m.shape[1] // up_slab
            up_ties = [zero_glu] + zeros[:n_up - 1]
            m = jnp.concatenate(
                [jnp.concatenate([_add_tile(m[:, c * up_slab:c * up_slab + LANES], up_ties[c]),
                                  m[:, c * up_slab + LANES:(c + 1) * up_slab]], axis=1)
                 for c in range(n_up)], axis=1)
        act = jnp.square(jnp.maximum(m, 0.0)).astype(BF16)
        res = jnp.dot(act, wdn_ref[...], preferred_element_type=F32)
        if zeros is None:
            o_ref[...] += res
            return
        slab = o_ref.shape[1] // len(zeros)
        first = n_up - 1
        o_ref[:, 0:first * slab] += res[:, 0:first * slab]
        for n in range(first, len(zeros)):
            c0 = n * slab
            o_ref[:, c0 + LANES:c0 + slab] += res[:, c0 + LANES:c0 + slab]
            o_ref[:, c0:c0 + LANES] += _add_tile(res[:, c0:c0 + LANES], zeros[n])

    def mlp_tail():
        @pl.when(f == pl.num_programs(1) - 1)
        def _():
            h2 = o_ref[...]
            ms = jnp.mean(h2 * h2, axis=-1, keepdims=True)
            o_ref[...] = h2 * lax.rsqrt(ms + EPS) * fn_ref[...]

    @pl.when(i == 0)
    def _():
        @pl.when(f == 0)
        def _():
            halo_scr[...] = jnp.zeros_like(halo_scr)
        conf_part()

    @pl.when(jnp.logical_and(i > 0, i < n_blocks))
    def _():
        mlp_head()
        mlp_body(*conf_part())
        mlp_tail()

    @pl.when(i == n_blocks)
    def _():
        mlp_head()
        mlp_body()
        mlp_tail()


def _mlp(x2, proj3, b_glu, dw_w, dw_b, ln_g, ln_b, dn_o, w_out_b, n2, w_up_b, w_dn_b, fn,
         *, seq, tm=512, rt=128):
    t, d = x2.shape
    dff = w_up_b.shape[1]
    nb = t // tm
    nf = N_GROUPS
    tf = dff // nf
    prev = lambda i: jnp.maximum(i - 1, 0)
    nxt = lambda i: jnp.minimum(i, nb - 1)
    vec = lambda off: pl.BlockSpec((1, 1, LANES), lambda i, f: (off + f, 0, 0))
    return pl.pallas_call(
        functools.partial(_mlp_kernel, tm=tm, rt=rt, blocks_per_seq=seq // tm),
        out_shape=jax.ShapeDtypeStruct((t, d), F32),
        grid=(nb + 1, nf),
        in_specs=[
            pl.BlockSpec((tm, d), lambda i, f: (prev(i), 0)),
            pl.BlockSpec((1, tm, LANES), lambda i, f: (CB_VAL + f, nxt(i), 0)),
            pl.BlockSpec((1, tm, LANES), lambda i, f: (CB_GATE + f, nxt(i), 0)),
            vec(0), vec(N_GROUPS),
            pl.BlockSpec((1, CONV_WIDTH, LANES), lambda i, f: (f, 0, 0)),
            vec(0), vec(0), vec(0),
            pl.BlockSpec((N_GROUPS, tm, LANES), lambda i, f: (0, prev(i), 0)),
            pl.BlockSpec((d, d), lambda i, f: (0, 0), pipeline_mode=pl.Buffered(1)),
            pl.BlockSpec((1, d), lambda i, f: (0, 0)),
            pl.BlockSpec((d, tf), lambda i, f: (0, jnp.where(i == 0, 0, f))),
            pl.BlockSpec((tf, d), lambda i, f: (jnp.where(i == 0, 0, f), 0)),
            pl.BlockSpec((1, d), lambda i, f: (0, 0)),
        ],
        out_specs=pl.BlockSpec((tm, d), lambda i, f: (prev(i), 0)),
        scratch_shapes=[
            pltpu.VMEM((tm, d), BF16),
            pltpu.VMEM((CONF_HALO + tm, LANES), F32),
            pltpu.VMEM((N_GROUPS, CONF_HALO, LANES), F32),
            pltpu.VMEM((2, N_GROUPS, tm, LANES), BF16),
        ],
        compiler_params=pltpu.CompilerParams(
            dimension_semantics=("arbitrary", "arbitrary"),
            vmem_limit_bytes=VMEM_LIMIT),
        name="mlp",
    )(x2, proj3, proj3, b_glu, b_glu, dw_w, dw_b, ln_g, ln_b, dn_o,
      w_out_b, n2, w_up_b, w_dn_b, fn)


def _lane_rep(vals):
    return jnp.broadcast_to(vals.astype(F32)[:, None], (vals.shape[0], LANES))


def kernel(x, norm1_w, w_in, b_glu, conf_dw_w, conf_dw_b, conf_ln_g, conf_ln_b, dn_conv_w,
           dn_a_log, dn_dt_bias, dn_norm_w, w_out, norm2_w, w_mlp_up, w_mlp_down, final_norm_w):
    batch, seq, d = x.shape
    assert norm1_w.shape[0] == 1, "single-layer block"
    h = x.reshape(batch * seq, d)

    d_in = w_in.shape[2]
    w_in_p = jnp.pad(w_in[0].astype(BF16), ((0, 0), (0, N_COLBLOCKS * LANES - d_in)))
    proj3 = _inproj(h, norm1_w[0].reshape(1, d), w_in_p)

    conv_w4 = dn_conv_w[0].reshape(SHORT_CONV_WIDTH, 3, N_GROUPS, LANES)
    dn_o, w_out_b, w_up_b, w_dn_b = _delta(
        proj3, conv_w4, _lane_rep(dn_a_log[0]), _lane_rep(dn_dt_bias[0]),
        dn_norm_w[0].reshape(1, LANES), (w_out[0], w_mlp_up[0], w_mlp_down[0]),
        batch=batch, seq=seq)

    grp = lambda v: v.reshape(-1, 1, LANES)
    dw_w = conf_dw_w[0].reshape(CONV_WIDTH, N_GROUPS, LANES).transpose(1, 0, 2)
    out = _mlp(h, proj3, grp(b_glu[0]), dw_w, grp(conf_dw_b[0]), grp(conf_ln_g[0]),
               grp(conf_ln_b[0]), dn_o, w_out_b, norm2_w[0].reshape(1, d), w_up_b, w_dn_b,
               final_norm_w.reshape(1, d), seq=seq)
    return out.reshape(batch, seq, d)
```
